```python
import jax, jax.numpy as jnp
from jax import lax
import numpy as np

D_MODEL = 1024
BATCH = 8
SEQ = 2048
DEPTH = 1

D_POOL = D_MODEL
POOL_WINDOWS = (2, 4, 8, 16)
N_POOL_GROUPS = len(POOL_WINDOWS)
POOL_GROUP = D_POOL // N_POOL_GROUPS
D_REC = D_MODEL
HEAD_DIM = 128
N_REC_HEADS = D_REC // HEAD_DIM
CHUNK = 64
D_MIX = D_POOL + D_REC
D_IN_PROJ = 2 * D_POOL + 4 * D_REC
EPS = 1e-6

kernel_name = "hybrid_pool_hgrn2_parallel_heads"


def rmsnorm(x, g):
    xf = x.astype(jnp.float32)
    y = xf * lax.rsqrt(jnp.mean(xf * xf, axis=-1, keepdims=True) + EPS)
    return y.astype(x.dtype) * g


def causal_multiscale_pool(u, pool_w, pool_scale):
    B, T, _ = u.shape
    ug = u.reshape(B, T, N_POOL_GROUPS, POOL_GROUP).astype(jnp.float32)
    cs0 = jnp.pad(jnp.cumsum(ug, axis=1), ((0, 0), (1, 0), (0, 0), (0, 0)))
    pos = jnp.arange(T, dtype=jnp.float32)
    pooled = []
    for gi, w in enumerate(POOL_WINDOWS):
        c = cs0[:, :, gi]
        hi = c[:, 1:]
        lo = jnp.pad(c[:, : T + 1 - w], ((0, 0), (w - 1, 0), (0, 0)))
        count = jnp.minimum(pos + 1.0, float(w))[None, :, None]
        pooled.append((hi - lo) / count)
    pooled = jnp.stack(pooled, axis=2)
    mixed = jnp.einsum('btgc,gcd->btgd', pooled - ug, pool_w.astype(jnp.float32))
    return mixed.reshape(B, T, D_POOL).astype(u.dtype) * pool_scale


def hgrn2_chunked(q, k, v, log_f):
    B, H, T, Dk = q.shape
    Dv = v.shape[-1]
    n = T // CHUNK

    def to_chunks(a):
        return a.reshape(B, H, n, CHUNK, a.shape[-1]).transpose(2, 0, 1, 3, 4)

    qc, kc, vc, gc = to_chunks(q), to_chunks(k), to_chunks(v), to_chunks(log_f)
    causal = jnp.tril(jnp.ones((CHUNK, CHUNK), dtype=bool))[:, :, None]

    def step(S, inp):
        qb, kb, vb, gb = inp
        G = jnp.cumsum(gb, axis=2)
        diff = G[:, :, :, None, :] - G[:, :, None, :, :]
        decay = jnp.exp(jnp.where(causal, diff, -jnp.inf))
        A = jnp.einsum('bhtd,bhsd,bhtsd->bhts', qb, kb, decay)
        o = (jnp.einsum('bhts,bhsv->bhtv', A, vb)
             + jnp.einsum('bhtd,bhdv->bhtv', qb * jnp.exp(G), S))
        G_last = G[:, :, -1]
        k_dec = kb * jnp.exp(G_last[:, :, None, :] - G)
        S = jnp.exp(G_last)[..., None] * S + jnp.einsum('bhsd,bhsv->bhdv', k_dec, vb)
        return S, o

    S0 = jnp.zeros((B, H, Dk, Dv), jnp.float32)
    _, o = lax.scan(step, S0, (qc, kc, vc, gc))
    return o.transpose(1, 2, 0, 3, 4).reshape(B, H, T, Dv)


def hybrid_layer(x, norm_g, w_in, pool_w, pool_scale, lb, rec_norm_g, w_out):
    B, T, _ = x.shape
    h = rmsnorm(x, norm_g)
    proj = jnp.einsum('btd,de->bte', h, w_in)
    o = 0
    pool_in = proj[..., o:o + D_POOL]; o += D_POOL
    pool_gate = proj[..., o:o + D_POOL]; o += D_POOL
    q = proj[..., o:o + D_REC]; o += D_REC
    f_logit = proj[..., o:o + D_REC]; o += D_REC
    i_in = proj[..., o:o + D_REC]; o += D_REC
    rec_gate = proj[..., o:o + D_REC]

    y_pool = causal_multiscale_pool(pool_in, pool_w, pool_scale) * jax.nn.silu(pool_gate)

    def heads(a):
        return a.astype(jnp.float32).reshape(B, T, N_REC_HEADS, HEAD_DIM).transpose(0, 2, 1, 3)

    f = lb + (1.0 - lb) * jax.nn.sigmoid(f_logit.astype(jnp.float32))
    rec = hgrn2_chunked(heads(jax.nn.silu(q)), heads(1.0 - f), heads(i_in), heads(jnp.log(f)))
    rec = rec.transpose(0, 2, 1, 3)
    rec = rec * lax.rsqrt(jnp.mean(rec * rec, axis=-1, keepdims=True) + EPS)
    rec = rec.reshape(B, T, D_REC).astype(x.dtype) * rec_norm_g
    y_rec = rec * jax.nn.silu(rec_gate)

    y = jnp.concatenate([y_pool, y_rec], axis=-1)
    return x + jnp.einsum('bte,ed->btd', y, w_out)


def setup_inputs(seed: int = 0) -> dict:
    key = jax.random.key(seed)
    ks = jax.random.split(key, 9)
    f32 = jnp.float32
    return {
        "x": jax.random.normal(ks[0], (BATCH, SEQ, D_MODEL), f32),
        "norm1_g": 1.0 + 0.02 * jax.random.normal(ks[1], (DEPTH, D_MODEL), f32),
        "w_in": jax.random.normal(ks[2], (DEPTH, D_MODEL, D_IN_PROJ), f32) * D_MODEL ** -0.5,
        "pool_w": jax.random.normal(ks[3], (DEPTH, N_POOL_GROUPS, POOL_GROUP, POOL_GROUP), f32) * POOL_GROUP ** -0.5,
        "pool_scale": 1.0 + 0.02 * jax.random.normal(ks[4], (DEPTH, D_POOL), f32),
        "lb_logits": 0.5 * jax.random.normal(ks[5], (DEPTH + 1, D_REC), f32),
        "rec_norm_g": 1.0 + 0.02 * jax.random.normal(ks[6], (DEPTH, D_REC), f32),
        "w_out": jax.random.normal(ks[7], (DEPTH, D_MIX, D_MODEL), f32) * D_MIX ** -0.5,
        "final_norm_g": 1.0 + 0.02 * jax.random.normal(ks[8], (D_MODEL,), f32),
    }


def reference(x, norm1_g, w_in, pool_w, pool_scale, lb_logits, rec_norm_g, w_out, final_norm_g):
    lb_all = jnp.cumsum(jax.nn.softmax(lb_logits.astype(jnp.float32), axis=0), axis=0)
    for layer in range(DEPTH):
        x = hybrid_layer(x, norm1_g[layer], w_in[layer], pool_w[layer], pool_scale[layer],
                         lb_all[layer], rec_norm_g[layer], w_out[layer])
    return rmsnorm(x, final_norm_g)
```

```python
import functools

import numpy as np
import jax
import jax.numpy as jnp
from jax import lax
from jax.experimental import pallas as pl
from jax.experimental.pallas import tpu as pltpu

D_MODEL = 1024
HEAD_DIM = 128
N_HEADS = D_MODEL // HEAD_DIM
POOL_WINDOWS = (2, 4, 8, 16)
POOL_GROUP = D_MODEL // len(POOL_WINDOWS)
EPS = 1e-6

CHUNK = 128
N_LEVELS = 7
HIST = 16
SUBLANES = 8
VMEM_LIMIT_BYTES = 56 * 1024 * 1024

F32 = jnp.float32
BF16 = jnp.bfloat16


def _sigmoid(z):
    return 1.0 / (1.0 + jnp.exp(-z))


def _silu(z):
    return z * _sigmoid(z)


def _nt_dot(a, b):
    return lax.dot_general(a, b, (((1,), (1,)), ((), ())), preferred_element_type=F32)


def _tn_dot(a, b):
    return lax.dot_general(a, b, (((0,), (0,)), ((), ())), preferred_element_type=F32)


def _ref_rows(g, m):
    rows, lanes = g.shape
    blk = 2 * m
    if blk >= SUBLANES:
        parts = [jnp.broadcast_to(g[b * blk + m - 1:b * blk + m, :], (blk, lanes))
                 for b in range(rows // blk)]
        return parts[0] if len(parts) == 1 else jnp.concatenate(parts, axis=0)
    g3 = g.reshape(rows // SUBLANES, SUBLANES, lanes)
    sub = lax.broadcasted_iota(jnp.int32, g3.shape, 1)
    r = None
    for jb in range(SUBLANES // blk):
        row = jb * blk + m - 1
        bc = jnp.broadcast_to(g3[:, row:row + 1, :], g3.shape)
        r = bc if r is None else jnp.where(sub >= jb * blk, bc, r)
    return r.reshape(rows, lanes)


def _block_kernel(x_ref, g1_ref, win_ref, pw_ref, ps_ref, lbl_ref, rng_ref, wout_ref, gf_ref,
                  tri_ref, lvl_ref, o_ref, hist_ref, st_ref, y_ref):
    tj = pl.program_id(1)
    tt = CHUNK
    d = D_MODEL

    @pl.when(tj == 0)
    def _():
        hist_ref[0:HIST, :] = jnp.zeros((HIST, d), F32)
        st_ref[...] = jnp.zeros_like(st_ref)

    x = x_ref[0]
    ms = jnp.mean(x * x, axis=-1, keepdims=True)
    hb = ((x * lax.rsqrt(ms + EPS)) * g1_ref[...]).astype(BF16)

    def proj(k):
        return jnp.dot(hb, win_ref[:, k * d:(k + 1) * d], preferred_element_type=F32)

    u = proj(0)
    pgate = proj(1)
    hist_ref[HIST:HIST + tt, :] = u
    pos = tj * tt + lax.broadcasted_iota(jnp.int32, (tt, 1), 0)
    for g, w in enumerate(POOL_WINDOWS):
        cs = slice(g * POOL_GROUP, (g + 1) * POOL_GROUP)
        ug = u[:, cs]
        acc = ug
        for j in range(1, w):
            acc = acc + hist_ref[HIST - j:HIST - j + tt, cs]
        cnt = jnp.minimum(pos + 1, w).astype(F32)
        pm = acc / cnt - ug
        mixed = jnp.dot(pm.astype(BF16), pw_ref[g], preferred_element_type=F32)
        y_ref[:, cs] = (mixed * ps_ref[:, cs] * _silu(pgate[:, cs])).astype(BF16)
    hist_ref[0:HIST, :] = hist_ref[tt:tt + HIST, :]

    lbl = lbl_ref[...]
    e = jnp.exp(lbl - jnp.max(lbl, axis=0, keepdims=True))
    lb = e[0:1, :] / jnp.sum(e, axis=0, keepdims=True)

    f = lb + (1.0 - lb) * _sigmoid(proj(3))
    lf = jnp.log(f)
    kk = 1.0 - f
    qs = _silu(proj(2))
    v = proj(4)
    rg = proj(5)

    h1 = lf.astype(BF16)
    r1 = lf - h1.astype(F32)
    h2 = r1.astype(BF16)
    h3 = (r1 - h2.astype(F32)).astype(BF16)
    tri = tri_ref[...]
    gcum = (jnp.dot(tri, h1, preferred_element_type=F32)
            + jnp.dot(tri, h2, preferred_element_type=F32)
            + jnp.dot(tri, h3, preferred_element_type=F32))

    lvl = lvl_ref[...]
    for h in range(N_HEADS):
        hs = slice(h * HEAD_DIM, (h + 1) * HEAD_DIM)
        gh = gcum[:, hs]
        qh = qs[:, hs]
        kh = kk[:, hs]
        vh = v[:, hs].astype(BF16)
        a = jnp.where(lvl == N_LEVELS, jnp.sum(qh * kh, axis=-1, keepdims=True), 0.0)
        for l in range(N_LEVELS):
            wgt = jnp.exp(-jnp.abs(gh - _ref_rows(gh, 1 << l)))
            al = _nt_dot((qh * wgt).astype(BF16), (kh * wgt).astype(BF16))
            a = jnp.where(lvl == l, al, a)
        st = st_ref[h]
        qhat = (qh * jnp.exp(gh)).astype(BF16)
        o = jnp.dot(a.astype(BF16), vh, preferred_element_type=F32) + _nt_dot(qhat, st.astype(BF16))
        gl = gh[tt - 1:tt, :]
        kdec = (kh * jnp.exp(gl - gh)).astype(BF16)
        st_ref[h] = st * jnp.exp(gl) + _tn_dot(vh, kdec)
        msr = jnp.mean(o * o, axis=-1, keepdims=True)
        rec = (o * lax.rsqrt(msr + EPS)) * rng_ref[:, hs]
        y_ref[:, d + h * HEAD_DIM:d + (h + 1) * HEAD_DIM] = (rec * _silu(rg[:, hs])).astype(BF16)

    yo = jnp.dot(y_ref[...], wout_ref[...], preferred_element_type=F32) + x
    ms2 = jnp.mean(yo * yo, axis=-1, keepdims=True)
    o_ref[0] = (yo * lax.rsqrt(ms2 + EPS)) * gf_ref[...]


def _level_table(n):
    t = np.arange(n)[:, None]
    s = np.arange(n)[None, :]
    x = t ^ s
    hb = np.floor(np.log2(np.maximum(x, 1))).astype(np.int32)
    return np.where(s < t, hb, np.where(s == t, N_LEVELS, -1)).astype(np.int32)


def _const_spec(shape):
    zeros = (0,) * len(shape)
    return pl.BlockSpec(shape, lambda b, j: zeros, pipeline_mode=pl.Buffered(1))


@jax.jit
def kernel(x, norm1_g, w_in, pool_w, pool_scale, lb_logits, rec_norm_g, w_out, final_norm_g):
    batch, seq, d = x.shape
    assert d == D_MODEL and seq % CHUNK == 0
    assert w_in.shape == (1, d, 6 * d) and w_out.shape == (1, 2 * d, d)
    tt = CHUNK
    win = w_in[0].astype(BF16)
    wout = w_out[0].astype(BF16)
    pw = pool_w[0].astype(BF16)
    tri = jnp.asarray(np.tril(np.ones((tt, tt), np.float32)), dtype=BF16)
    lvl = jnp.asarray(_level_table(tt))

    return pl.pallas_call(
        _block_kernel,
        grid=(batch, seq // tt),
        in_specs=[
            pl.BlockSpec((1, tt, d), lambda b, j: (b, j, 0)),
            _const_spec((1, d)),
            _const_spec((d, 6 * d)),
            _const_spec((len(POOL_WINDOWS), POOL_GROUP, POOL_GROUP)),
            _const_spec((1, d)),
            _const_spec(lb_logits.shape),
            _const_spec((1, d)),
            _const_spec((2 * d, d)),
            _const_spec((1, d)),
            _const_spec((tt, tt)),
            _const_spec((tt, tt)),
        ],
        out_specs=pl.BlockSpec((1, tt, d), lambda b, j: (b, j, 0)),
        out_shape=jax.ShapeDtypeStruct(x.shape, x.dtype),
        scratch_shapes=[
            pltpu.VMEM((HIST + tt, d), F32),
            pltpu.VMEM((N_HEADS, HEAD_DIM, HEAD_DIM), F32),
            pltpu.VMEM((tt, 2 * d), BF16),
        ],
        compiler_params=pltpu.CompilerParams(
            dimension_semantics=("arbitrary", "arbitrary"),
            vmem_limit_bytes=VMEM_LIMIT_BYTES),
        name="hybrid_pool_hgrn2_block",
    )(x, norm1_g, win, pw, pool_scale, lb_logits, rec_norm_g, wout, final_norm_g.reshape(1, d), tri, lvl)
```

```python
import functools

import numpy as np
import jax
import jax.numpy as jnp
from jax import lax
from jax.experimental import pallas as pl
from jax.experimental.pallas import tpu as pltpu

D_MODEL = 1024
HEAD_DIM = 128
N_HEADS = D_MODEL // HEAD_DIM
POOL_WINDOWS = (2, 4, 8, 16)
POOL_GROUP = D_MODEL // len(POOL_WINDOWS)
EPS = 1e-6

CHUNK = 128
N_LEVELS = 7
HIST = 16
SUBLANES = 8
VMEM_LIMIT_BYTES = 56 * 1024 * 1024

F32 = jnp.float32
BF16 = jnp.bfloat16


def _sigmoid(z):
    return 1.0 / (1.0 + jnp.exp(-z))


def _silu(z):
    return z * _sigmoid(z)


def _nt_dot(a, b):
    return lax.dot_general(a, b, (((1,), (1,)), ((), ())), preferred_element_type=F32)


def _tn_dot(a, b):
    return lax.dot_general(a, b, (((0,), (0,)), ((), ())), preferred_element_type=F32)


def _ref_rows(g, m):
    rows, lanes = g.shape
    blk = 2 * m
    if blk >= SUBLANES:
        parts = [jnp.broadcast_to(g[b * blk + m - 1:b * blk + m, :], (blk, lanes))
                 for b in range(rows // blk)]
        return parts[0] if len(parts) == 1 else jnp.concatenate(parts, axis=0)
    g3 = g.reshape(rows // SUBLANES, SUBLANES, lanes)
    sub = lax.broadcasted_iota(jnp.int32, g3.shape, 1)
    r = None
    for jb in range(SUBLANES // blk):
        row = jb * blk + m - 1
        bc = jnp.broadcast_to(g3[:, row:row + 1, :], g3.shape)
        r = bc if r is None else jnp.where(sub >= jb * blk, bc, r)
    return r.reshape(rows, lanes)


def _block_kernel(x_ref, g1_ref, win_in_ref, pw_in_ref, ps_ref, lbl_ref, rng_ref, wout_in_ref, gf_ref,
                  tri_ref, lvl_ref, o_ref, hist_ref, st_ref, y_ref, win_ref, pw_ref, wout_ref):
    tj = pl.program_id(1)
    tt = CHUNK
    d = D_MODEL

    @pl.when((pl.program_id(0) == 0) & (tj == 0))
    def _():
        for k in range(6):
            win_ref[:, k * d:(k + 1) * d] = win_in_ref[:, k * d:(k + 1) * d]
        wout_ref[...] = wout_in_ref[...]
        pw_ref[...] = pw_in_ref[...]

    @pl.when(tj == 0)
    def _():
        hist_ref[0:HIST, :] = jnp.zeros((HIST, d), F32)
        st_ref[...] = jnp.zeros_like(st_ref)

    x = x_ref[0]
    ms = jnp.mean(x * x, axis=-1, keepdims=True)
    hb = ((x * lax.rsqrt(ms + EPS)) * g1_ref[...]).astype(BF16)

    def proj(k):
        return jnp.dot(hb, win_ref[:, k * d:(k + 1) * d], preferred_element_type=F32)

    u = proj(0)
    pgate = proj(1)
    hist_ref[HIST:HIST + tt, :] = u
    pos = tj * tt + lax.broadcasted_iota(jnp.int32, (tt, 1), 0)
    for g, w in enumerate(POOL_WINDOWS):
        cs = slice(g * POOL_GROUP, (g + 1) * POOL_GROUP)
        ug = u[:, cs]
        acc = ug
        for j in range(1, w):
            acc = acc + hist_ref[HIST - j:HIST - j + tt, cs]
        cnt = jnp.minimum(pos + 1, w).astype(F32)
        pm = acc / cnt - ug
        mixed = jnp.dot(pm.astype(BF16), pw_ref[g], preferred_element_type=F32)
        y_ref[:, cs] = (mixed * ps_ref[:, cs] * _silu(pgate[:, cs])).astype(BF16)
    hist_ref[0:HIST, :] = hist_ref[tt:tt + HIST, :]

    lbl = lbl_ref[...]
    e = jnp.exp(lbl - jnp.max(lbl, axis=0, keepdims=True))
    lb = e[0:1, :] / jnp.sum(e, axis=0, keepdims=True)

    f = lb + (1.0 - lb) * _sigmoid(proj(3))
    lf = jnp.log(f)
    kk = 1.0 - f
    qs = _silu(proj(2))
    v = proj(4)
    rg = proj(5)

    h1 = lf.astype(BF16)
    r1 = lf - h1.astype(F32)
    h2 = r1.astype(BF16)
    h3 = (r1 - h2.astype(F32)).astype(BF16)
    tri = tri_ref[...]
    gcum = (jnp.dot(tri, h1, preferred_element_type=F32)
            + jnp.dot(tri, h2, preferred_element_type=F32)
            + jnp.dot(tri, h3, preferred_element_type=F32))

    lvl = lvl_ref[...]
    for h in range(N_HEADS):
        hs = slice(h * HEAD_DIM, (h + 1) * HEAD_DIM)
        gh = gcum[:, hs]
        qh = qs[:, hs]
        kh = kk[:, hs]
        vh = v[:, hs].astype(BF16)
        a = jnp.where(lvl == N_LEVELS, jnp.sum(qh * kh, axis=-1, keepdims=True), 0.0)
        for l in range(N_LEVELS):
            wgt = jnp.exp(-jnp.abs(gh - _ref_rows(gh, 1 << l)))
            al = _nt_dot((qh * wgt).astype(BF16), (kh * wgt).astype(BF16))
            a = jnp.where(lvl == l, al, a)
        st = st_ref[h]
        qhat = (qh * jnp.exp(gh)).astype(BF16)
        o = jnp.dot(a.astype(BF16), vh, preferred_element_type=F32) + _nt_dot(qhat, st.astype(BF16))
        gl = gh[tt - 1:tt, :]
        kdec = (kh * jnp.exp(gl - gh)).astype(BF16)
        st_ref[h] = st * jnp.exp(gl) + _tn_dot(vh, kdec)
        msr = jnp.mean(o * o, axis=-1, keepdims=True)
        rec = (o * lax.rsqrt(msr + EPS)) * rng_ref[:, hs]
        y_ref[:, d + h * HEAD_DIM:d + (h + 1) * HEAD_DIM] = (rec * _silu(rg[:, hs])).astype(BF16)

    yo = jnp.dot(y_ref[...], wout_ref[...], preferred_element_type=F32) + x
    ms2 = jnp.mean(yo * yo, axis=-1, keepdims=True)
    o_ref[0] = (yo * lax.rsqrt(ms2 + EPS)) * gf_ref[...]


def _level_table(n):
    t = np.arange(n)[:, None]
    s = np.arange(n)[None, :]
    x = t ^ s
    hb = np.floor(np.log2(np.maximum(x, 1))).astype(np.int32)
    return np.where(s < t, hb, np.where(s == t, N_LEVELS, -1)).astype(np.int32)


def _const_spec(shape):
    zeros = (0,) * len(shape)
    return pl.BlockSpec(shape, lambda b, j: zeros, pipeline_mode=pl.Buffered(1))


@jax.jit
def kernel(x, norm1_g, w_in, pool_w, pool_scale, lb_logits, rec_norm_g, w_out, final_norm_g):
    batch, seq, d = x.shape
    assert d == D_MODEL and seq % CHUNK == 0
    assert w_in.shape == (1, d, 6 * d) and w_out.shape == (1, 2 * d, d)
    tt = CHUNK
    win = w_in[0].astype(BF16)
    wout = w_out[0].astype(BF16)
    pw = pool_w[0].astype(BF16)
    tri = jnp.asarray(np.tril(np.ones((tt, tt), np.float32)), dtype=BF16)
    lvl = jnp.asarray(_level_table(tt))

    return pl.pallas_call(
        _block_kernel,
        grid=(batch, seq // tt),
        in_specs=[
            pl.BlockSpec((1, tt, d), lambda b, j: (b, j, 0)),
            _const_spec((1, d)),
            _const_spec((d, 6 * d)),
            _const_spec((len(POOL_WINDOWS), POOL_GROUP, POOL_GROUP)),
            _const_spec((1, d)),
            _const_spec(lb_logits.shape),
            _const_spec((1, d)),
            _const_spec((2 * d, d)),
            _const_spec((1, d)),
            _const_spec((tt, tt)),
            _const_spec((tt, tt)),
        ],
        out_specs=pl.BlockSpec((1, tt, d), lambda b, j: (b, j, 0)),
        out_shape=jax.ShapeDtypeStruct(x.shape, x.dtype),
        scratch_shapes=[
            pltpu.VMEM((HIST + tt, d), F32),
            pltpu.VMEM((N_HEADS, HEAD_DIM, HEAD_DIM), F32),
            pltpu.VMEM((tt, 2 * d), BF16),
            pltpu.VMEM((d, 6 * d), BF16),
            pltpu.VMEM((len(POOL_WINDOWS), POOL_GROUP, POOL_GROUP), BF16),
            pltpu.VMEM((2 * d, d), BF16),
        ],
        compiler_params=pltpu.CompilerParams(
            dimension_semantics=("arbitrary", "arbitrary"),
            vmem_limit_bytes=VMEM_LIMIT_BYTES),
        name="hybrid_pool_hgrn2_block",
    )(x, norm1_g, win, pw, pool_scale, lb_logits, rec_norm_g, wout, final_norm_g.reshape(1, d), tri, lvl)
```

```python
import functools

import numpy as np
import jax
import jax.numpy as jnp
from jax import lax
from jax.experimental import pallas as pl
from jax.experimental.pallas import tpu as pltpu

D_MODEL = 1024
HEAD_DIM = 128
N_HEADS = D_MODEL // HEAD_DIM
POOL_WINDOWS = (2, 4, 8, 16)
POOL_GROUP = D_MODEL // len(POOL_WINDOWS)
EPS = 1e-6

CHUNK = 128
N_LEVELS = 7
TILE = 256
HIST = 16
SUBLANES = 8
VMEM_LIMIT_BYTES = 56 * 1024 * 1024

F32 = jnp.float32
BF16 = jnp.bfloat16


def _sigmoid(z):
    return 1.0 / (1.0 + jnp.exp(-z))


def _silu(z):
    return z * _sigmoid(z)


def _nt_dot(a, b):
    return lax.dot_general(a, b, (((1,), (1,)), ((), ())), preferred_element_type=F32)


def _tn_dot(a, b):
    return lax.dot_general(a, b, (((0,), (0,)), ((), ())), preferred_element_type=F32)


def _ref_rows(g, m):
    rows, lanes = g.shape
    blk = 2 * m
    if blk >= SUBLANES:
        parts = [jnp.broadcast_to(g[b * blk + m - 1:b * blk + m, :], (blk, lanes))
                 for b in range(rows // blk)]
        return parts[0] if len(parts) == 1 else jnp.concatenate(parts, axis=0)
    g3 = g.reshape(rows // SUBLANES, SUBLANES, lanes)
    sub = lax.broadcasted_iota(jnp.int32, g3.shape, 1)
    r = None
    for jb in range(SUBLANES // blk):
        row = jb * blk + m - 1
        bc = jnp.broadcast_to(g3[:, row:row + 1, :], g3.shape)
        r = bc if r is None else jnp.where(sub >= jb * blk, bc, r)
    return r.reshape(rows, lanes)


def _block_kernel(x_ref, g1_ref, win_in_ref, pw_in_ref, ps_ref, lbl_ref, rng_ref, wout_in_ref, gf_ref,
                  tri_ref, lvl_ref, o_ref, hist_ref, st_ref, y_ref, win_ref, pw_ref, wout_ref):
    tj = pl.program_id(1)
    tt = TILE
    d = D_MODEL

    @pl.when((pl.program_id(0) == 0) & (tj == 0))
    def _():
        for k in range(6):
            win_ref[:, k * d:(k + 1) * d] = win_in_ref[:, k * d:(k + 1) * d]
        wout_ref[...] = wout_in_ref[...]
        pw_ref[...] = pw_in_ref[...]

    @pl.when(tj == 0)
    def _():
        hist_ref[0:HIST, :] = jnp.zeros((HIST, d), F32)
        st_ref[...] = jnp.zeros_like(st_ref)

    x = x_ref[0]
    ms = jnp.mean(x * x, axis=-1, keepdims=True)
    hb = ((x * lax.rsqrt(ms + EPS)) * g1_ref[...]).astype(BF16)

    def proj(k):
        return jnp.dot(hb, win_ref[:, k * d:(k + 1) * d], preferred_element_type=F32)

    u = proj(0)
    pgate = proj(1)
    hist_ref[HIST:HIST + tt, :] = u
    pos = tj * tt + lax.broadcasted_iota(jnp.int32, (tt, 1), 0)
    for g, w in enumerate(POOL_WINDOWS):
        cs = slice(g * POOL_GROUP, (g + 1) * POOL_GROUP)
        ug = u[:, cs]
        acc = ug
        for j in range(1, w):
            acc = acc + hist_ref[HIST - j:HIST - j + tt, cs]
        cnt = jnp.minimum(pos + 1, w).astype(F32)
        pm = acc / cnt - ug
        mixed = jnp.dot(pm.astype(BF16), pw_ref[g], preferred_element_type=F32)
        y_ref[:, cs] = (mixed * ps_ref[:, cs] * _silu(pgate[:, cs])).astype(BF16)
    hist_ref[0:HIST, :] = hist_ref[tt:tt + HIST, :]

    lbl = lbl_ref[...]
    e = jnp.exp(lbl - jnp.max(lbl, axis=0, keepdims=True))
    lb = e[0:1, :] / jnp.sum(e, axis=0, keepdims=True)

    f = lb + (1.0 - lb) * _sigmoid(proj(3))
    lf = jnp.log(f)
    kk = 1.0 - f
    qs = _silu(proj(2))
    v = proj(4)
    rg = proj(5)

    h1 = lf.astype(BF16)
    r1 = lf - h1.astype(F32)
    h2 = r1.astype(BF16)
    h3 = (r1 - h2.astype(F32)).astype(BF16)
    tri = tri_ref[...]
    lvl = lvl_ref[...]
    for c in range(tt // CHUNK):
        rows = slice(c * CHUNK, (c + 1) * CHUNK)
        gcum = (jnp.dot(tri, h1[rows], preferred_element_type=F32)
                + jnp.dot(tri, h2[rows], preferred_element_type=F32)
                + jnp.dot(tri, h3[rows], preferred_element_type=F32))
        for h in range(N_HEADS):
            hs = slice(h * HEAD_DIM, (h + 1) * HEAD_DIM)
            gh = gcum[:, hs]
            qh = qs[rows, hs]
            kh = kk[rows, hs]
            vh = v[rows, hs].astype(BF16)
            a = jnp.where(lvl == N_LEVELS, jnp.sum(qh * kh, axis=-1, keepdims=True), 0.0)
            for l in range(N_LEVELS):
                wgt = jnp.exp(-jnp.abs(gh - _ref_rows(gh, 1 << l)))
                al = _nt_dot((qh * wgt).astype(BF16), (kh * wgt).astype(BF16))
                a = jnp.where(lvl == l, al, a)
            st = st_ref[h]
            qhat = (qh * jnp.exp(gh)).astype(BF16)
            o = jnp.dot(a.astype(BF16), vh, preferred_element_type=F32) + _nt_dot(qhat, st.astype(BF16))
            gl = gh[CHUNK - 1:CHUNK, :]
            kdec = (kh * jnp.exp(gl - gh)).astype(BF16)
            st_ref[h] = st * jnp.exp(gl) + _tn_dot(vh, kdec)
            msr = jnp.mean(o * o, axis=-1, keepdims=True)
            rec = (o * lax.rsqrt(msr + EPS)) * rng_ref[:, hs]
            y_ref[rows, d + h * HEAD_DIM:d + (h + 1) * HEAD_DIM] = (rec * _silu(rg[rows, hs])).astype(BF16)

    yo = jnp.dot(y_ref[...], wout_ref[...], preferred_element_type=F32) + x
    ms2 = jnp.mean(yo * yo, axis=-1, keepdims=True)
    o_ref[0] = (yo * lax.rsqrt(ms2 + EPS)) * gf_ref[...]


def _level_table(n):
    t = np.arange(n)[:, None]
    s = np.arange(n)[None, :]
    x = t ^ s
    hb = np.floor(np.log2(np.maximum(x, 1))).astype(np.int32)
    return np.where(s < t, hb, np.where(s == t, N_LEVELS, -1)).astype(np.int32)


def _const_spec(shape):
    zeros = (0,) * len(shape)
    return pl.BlockSpec(shape, lambda b, j: zeros, pipeline_mode=pl.Buffered(1))


@jax.jit
def kernel(x, norm1_g, w_in, pool_w, pool_scale, lb_logits, rec_norm_g, w_out, final_norm_g):
    batch, seq, d = x.shape
    assert d == D_MODEL and seq % TILE == 0 and TILE % CHUNK == 0
    assert w_in.shape == (1, d, 6 * d) and w_out.shape == (1, 2 * d, d)
    tt = TILE
    win = w_in[0].astype(BF16)
    wout = w_out[0].astype(BF16)
    pw = pool_w[0].astype(BF16)
    tri = jnp.asarray(np.tril(np.ones((CHUNK, CHUNK), np.float32)), dtype=BF16)
    lvl = jnp.asarray(_level_table(CHUNK))

    return pl.pallas_call(
        _block_kernel,
        grid=(batch, seq // tt),
        in_specs=[
            pl.BlockSpec((1, tt, d), lambda b, j: (b, j, 0)),
            _const_spec((1, d)),
            _const_spec((d, 6 * d)),
            _const_spec((len(POOL_WINDOWS), POOL_GROUP, POOL_GROUP)),
            _const_spec((1, d)),
            _const_spec(lb_logits.shape),
            _const_spec((1, d)),
            _const_spec((2 * d, d)),
            _const_spec((1, d)),
            _const_spec((CHUNK, CHUNK)),
            _const_spec((CHUNK, CHUNK)),
        ],
        out_specs=pl.BlockSpec((1, tt, d), lambda b, j: (b, j, 0)),
        out_shape=jax.ShapeDtypeStruct(x.shape, x.dtype),
        scratch_shapes=[
            pltpu.VMEM((HIST + tt, d), F32),
            pltpu.VMEM((N_HEADS, HEAD_DIM, HEAD_DIM), F32),
            pltpu.VMEM((tt, 2 * d), BF16),
            pltpu.VMEM((d, 6 * d), BF16),
            pltpu.VMEM((len(POOL_WINDOWS), POOL_GROUP, POOL_GROUP), BF16),
            pltpu.VMEM((2 * d, d), BF16),
        ],
        compiler_params=pltpu.CompilerParams(
            dimension_semantics=("arbitrary", "arbitrary"),
            vmem_limit_bytes=VMEM_LIMIT_BYTES),
        name="hybrid_pool_hgrn2_block",
    )(x, norm1_g, win, pw, pool_scale, lb_logits, rec_norm_g, wout, final_norm_g.reshape(1, d), tri, lvl)
```

```python
import numpy as np
import jax
import jax.numpy as jnp
from jax import lax
from jax.experimental import pallas as pl
from jax.experimental.pallas import tpu as pltpu

D_MODEL = 1024
HEAD_DIM = 128
N_HEADS = D_MODEL // HEAD_DIM
POOL_WINDOWS = (2, 4, 8, 16)
N_GROUPS = len(POOL_WINDOWS)
POOL_GROUP = D_MODEL // N_GROUPS
N_PROJ = 6
EPS = 1e-6
NEG_LOG2E = -1.4426950408889634

CHUNK = 128
N_LEVELS = 7
TILE = 256
HIST = 16
SUBLANES = 8
ROWS_PER_WORD = 2
VMEM_LIMIT_BYTES = 56 * 1024 * 1024

F32 = jnp.float32
BF16 = jnp.bfloat16


def _sigmoid(z):
    return 1.0 / (1.0 + jnp.exp2(z * NEG_LOG2E))


def _silu(z):
    return z * _sigmoid(z)


def _dot(a, b):
    return jnp.dot(a, b, preferred_element_type=F32)


def _nt_dot(a, b):
    return lax.dot_general(a, b, (((1,), (1,)), ((), ())), preferred_element_type=F32)


def _tn_dot(a, b):
    return lax.dot_general(a, b, (((0,), (0,)), ((), ())), preferred_element_type=F32)


def _pack_rows(w):
    wb = w.astype(BF16)
    *lead, k, n = wb.shape
    pairs = jnp.swapaxes(wb.reshape(*lead, k // ROWS_PER_WORD, ROWS_PER_WORD, n), -1, -2)
    return lax.bitcast_convert_type(pairs, jnp.uint32)


def _unpack_rows(w_ref, rows, cols):
    assert rows.start % ROWS_PER_WORD == 0 and rows.stop % ROWS_PER_WORD == 0
    return pltpu.bitcast(w_ref[rows.start // ROWS_PER_WORD:rows.stop // ROWS_PER_WORD, cols], BF16)


def _level_operands(q, k, f, g2, level):
    m = 1 << level
    blk = 2 * m
    rows, lanes = g2.shape
    if level == 0:
        return q * f, k
    if blk > SUBLANES:
        qs, ks = [], []
        for b in range(rows // blk):
            lo = slice(b * blk, b * blk + m)
            hi = slice(b * blk + m, (b + 1) * blk)
            mid = g2[b * blk + m - 1:b * blk + m, :]
            qs += [jnp.zeros((m, lanes), F32), q[hi] * jnp.exp2(g2[hi] - mid)]
            ks += [k[lo] * jnp.exp2(mid - g2[lo]), jnp.zeros((m, lanes), F32)]
        return jnp.concatenate(qs, axis=0), jnp.concatenate(ks, axis=0)
    g3 = g2.reshape(rows // SUBLANES, SUBLANES, lanes)
    sub = lax.broadcasted_iota(jnp.int32, g3.shape, 1)
    mid = None
    for jb in range(SUBLANES // blk):
        row = jb * blk + m - 1
        bc = jnp.broadcast_to(g3[:, row:row + 1, :], g3.shape)
        mid = bc if mid is None else jnp.where(sub >= jb * blk, bc, mid)
    wgt = jnp.exp2(-jnp.abs(g3 - mid)).reshape(rows, lanes)
    return q * wgt, k * wgt


def _block_kernel(x_ref, g1_ref, win_ref, pw_ref, ps_ref, lbl_ref, rng_ref, wout_ref, gf_ref,
                  tri_ref, lvl_ref, o_ref, hist_ref, st_ref, y_ref):
    tj = pl.program_id(1)
    tt = TILE
    d = D_MODEL
    all_k = slice(0, d)

    @pl.when(tj == 0)
    def _():
        hist_ref[0:HIST, :] = jnp.zeros((HIST, d), F32)
        st_ref[...] = jnp.zeros_like(st_ref)

    x = x_ref[0]
    ms = jnp.mean(x * x, axis=-1, keepdims=True)
    hb = ((x * lax.rsqrt(ms + EPS)) * g1_ref[...]).astype(BF16)

    def proj(k):
        return _dot(hb, _unpack_rows(win_ref, all_k, slice(k * d, (k + 1) * d)))

    hist_ref[HIST:HIST + tt, :] = proj(0)
    pgate = proj(1)
    pos = tj * tt + lax.broadcasted_iota(jnp.int32, (tt, 1), 0)
    for g, w in enumerate(POOL_WINDOWS):
        cs = slice(g * POOL_GROUP, (g + 1) * POOL_GROUP)
        ext = hist_ref[:, cs]
        acc = ext
        span = 1
        while span < w:
            acc = acc + pltpu.roll(acc, span, 0)
            span *= 2
        ug = ext[HIST:HIST + tt]
        cnt = jnp.minimum(pos + 1, w).astype(F32)
        pm = acc[HIST:HIST + tt] / cnt - ug
        mixed = _dot(pm.astype(BF16), pltpu.bitcast(pw_ref[g], BF16))
        y_ref[:, cs] = (mixed * ps_ref[:, cs] * _silu(pgate[:, cs])).astype(BF16)
    hist_ref[0:HIST, :] = hist_ref[tt:tt + HIST, :]

    lbl = lbl_ref[...]
    e = jnp.exp(lbl - jnp.max(lbl, axis=0, keepdims=True))
    lb = e[0:1, :] / jnp.sum(e, axis=0, keepdims=True)

    f = lb + (1.0 - lb) * _sigmoid(proj(3))
    lf2 = jnp.log2(f)
    kk = 1.0 - f
    qs = _silu(proj(2))
    v = proj(4)
    rg = proj(5)

    h1 = lf2.astype(BF16)
    r1 = lf2 - h1.astype(F32)
    h2 = r1.astype(BF16)
    h3 = (r1 - h2.astype(F32)).astype(BF16)
    tri = tri_ref[...]
    lvl = lvl_ref[...]
    for c in range(tt // CHUNK):
        rows = slice(c * CHUNK, (c + 1) * CHUNK)
        gcum = _dot(tri, h1[rows]) + _dot(tri, h2[rows]) + _dot(tri, h3[rows])
        for h in range(N_HEADS):
            hs = slice(h * HEAD_DIM, (h + 1) * HEAD_DIM)
            g2 = gcum[:, hs]
            qh = qs[rows, hs]
            kh = kk[rows, hs]
            vh = v[rows, hs].astype(BF16)
            a = jnp.where(lvl == N_LEVELS, jnp.sum(qh * kh, axis=-1, keepdims=True), 0.0)
            for l in range(N_LEVELS):
                ql, kl = _level_operands(qh, kh, f[rows, hs], g2, l)
                a = jnp.where(lvl == l, _nt_dot(ql.astype(BF16), kl.astype(BF16)), a)
            st = st_ref[h]
            qhat = (qh * jnp.exp2(g2)).astype(BF16)
            o = _dot(a.astype(BF16), vh) + _nt_dot(qhat, st.astype(BF16))
            gl = g2[CHUNK - 1:CHUNK, :]
            kdec = (kh * jnp.exp2(gl - g2)).astype(BF16)
            st_ref[h] = st * jnp.exp2(gl) + _tn_dot(vh, kdec)
            msr = jnp.mean(o * o, axis=-1, keepdims=True)
            rec = (o * lax.rsqrt(msr + EPS)) * rng_ref[:, hs]
            y_ref[rows, d + h * HEAD_DIM:d + (h + 1) * HEAD_DIM] = (rec * _silu(rg[rows, hs])).astype(BF16)

    yo = _dot(y_ref[...], _unpack_rows(wout_ref, slice(0, 2 * d), all_k)) + x
    ms2 = jnp.mean(yo * yo, axis=-1, keepdims=True)
    o_ref[0] = (yo * lax.rsqrt(ms2 + EPS)) * gf_ref[...]


def _level_table(n):
    t = np.arange(n)[:, None]
    s = np.arange(n)[None, :]
    x = t ^ s
    hb = np.floor(np.log2(np.maximum(x, 1))).astype(np.int32)
    return np.where(s < t, hb, np.where(s == t, N_LEVELS, -1)).astype(np.int32)


def _const_spec(shape):
    zeros = (0,) * len(shape)
    return pl.BlockSpec(shape, lambda b, j: zeros, pipeline_mode=pl.Buffered(1))


@jax.jit
def kernel(x, norm1_g, w_in, pool_w, pool_scale, lb_logits, rec_norm_g, w_out, final_norm_g):
    batch, seq, d = x.shape
    assert d == D_MODEL and seq % TILE == 0 and TILE % CHUNK == 0
    assert w_in.shape == (1, d, N_PROJ * d) and w_out.shape == (1, 2 * d, d)
    tt = TILE
    win = _pack_rows(w_in[0])
    wout = _pack_rows(w_out[0])
    pw = _pack_rows(pool_w[0])
    tri = jnp.asarray(np.tril(np.ones((CHUNK, CHUNK), np.float32)), dtype=BF16)
    lvl = jnp.asarray(_level_table(CHUNK))

    return pl.pallas_call(
        _block_kernel,
        grid=(batch, seq // tt),
        in_specs=[
            pl.BlockSpec((1, tt, d), lambda b, j: (b, j, 0)),
            _const_spec((1, d)),
            _const_spec(win.shape),
            _const_spec(pw.shape),
            _const_spec((1, d)),
            _const_spec(lb_logits.shape),
            _const_spec((1, d)),
            _const_spec(wout.shape),
            _const_spec((1, d)),
            _const_spec((CHUNK, CHUNK)),
            _const_spec((CHUNK, CHUNK)),
        ],
        out_specs=pl.BlockSpec((1, tt, d), lambda b, j: (b, j, 0)),
        out_shape=jax.ShapeDtypeStruct(x.shape, x.dtype),
        scratch_shapes=[
            pltpu.VMEM((HIST + tt, d), F32),
            pltpu.VMEM((N_HEADS, HEAD_DIM, HEAD_DIM), F32),
            pltpu.VMEM((tt, 2 * d), BF16),
        ],
        compiler_params=pltpu.CompilerParams(
            dimension_semantics=("arbitrary", "arbitrary"),
            vmem_limit_bytes=VMEM_LIMIT_BYTES),
        name="hybrid_pool_hgrn2_block",
    )(x, norm1_g, win, pw, pool_scale, lb_logits, rec_norm_g, wout, final_norm_g.reshape(1, d), tri, lvl)
```

```python
import numpy as np
import jax
import jax.numpy as jnp
from jax import lax
from jax.experimental import pallas as pl
from jax.experimental.pallas import tpu as pltpu

D_MODEL = 1024
HEAD_DIM = 128
N_HEADS = D_MODEL // HEAD_DIM
POOL_WINDOWS = (2, 4, 8, 16)
N_GROUPS = len(POOL_WINDOWS)
POOL_GROUP = D_MODEL // N_GROUPS
N_PROJ = 6
EPS = 1e-6
NEG_LOG2E = -1.4426950408889634

CHUNK = 128
N_LEVELS = 7
TILE = 256
HIST = 16
SUBLANES = 8
ROWS_PER_WORD = 2
PACK_BLOCK_ROWS = 256
VMEM_LIMIT_BYTES = 56 * 1024 * 1024

F32 = jnp.float32
BF16 = jnp.bfloat16


def _sigmoid(z):
    return 1.0 / (1.0 + jnp.exp2(z * NEG_LOG2E))


def _silu(z):
    return z * _sigmoid(z)


def _dot(a, b):
    return jnp.dot(a, b, preferred_element_type=F32)


def _nt_dot(a, b):
    return lax.dot_general(a, b, (((1,), (1,)), ((), ())), preferred_element_type=F32)


def _tn_dot(a, b):
    return lax.dot_general(a, b, (((0,), (0,)), ((), ())), preferred_element_type=F32)


def _pack_kernel(w_ref, o_ref):
    o_ref[...] = pltpu.bitcast(w_ref[...].astype(BF16), jnp.uint32)


def _pack_rows(w):
    k, n = w.shape
    assert k % PACK_BLOCK_ROWS == 0
    return pl.pallas_call(
        _pack_kernel,
        grid=(k // PACK_BLOCK_ROWS,),
        in_specs=[pl.BlockSpec((PACK_BLOCK_ROWS, n), lambda i: (i, 0))],
        out_specs=pl.BlockSpec((PACK_BLOCK_ROWS // ROWS_PER_WORD, n), lambda i: (i, 0)),
        out_shape=jax.ShapeDtypeStruct((k // ROWS_PER_WORD, n), jnp.uint32),
        name="pack_weight_rows",
    )(w)


def _unpack_rows(w_ref, rows, cols):
    assert rows.start % ROWS_PER_WORD == 0 and rows.stop % ROWS_PER_WORD == 0
    return pltpu.bitcast(w_ref[rows.start // ROWS_PER_WORD:rows.stop // ROWS_PER_WORD, cols], BF16)


def _level_operands(q, k, f, g2, level):
    m = 1 << level
    blk = 2 * m
    rows, lanes = g2.shape
    if level == 0:
        return q * f, k
    if blk > SUBLANES:
        qs, ks = [], []
        for b in range(rows // blk):
            lo = slice(b * blk, b * blk + m)
            hi = slice(b * blk + m, (b + 1) * blk)
            mid = g2[b * blk + m - 1:b * blk + m, :]
            qs += [jnp.zeros((m, lanes), F32), q[hi] * jnp.exp2(g2[hi] - mid)]
            ks += [k[lo] * jnp.exp2(mid - g2[lo]), jnp.zeros((m, lanes), F32)]
        return jnp.concatenate(qs, axis=0), jnp.concatenate(ks, axis=0)
    g3 = g2.reshape(rows // SUBLANES, SUBLANES, lanes)
    sub = lax.broadcasted_iota(jnp.int32, g3.shape, 1)
    mid = None
    for jb in range(SUBLANES // blk):
        row = jb * blk + m - 1
        bc = jnp.broadcast_to(g3[:, row:row + 1, :], g3.shape)
        mid = bc if mid is None else jnp.where(sub >= jb * blk, bc, mid)
    wgt = jnp.exp2(-jnp.abs(g3 - mid)).reshape(rows, lanes)
    return q * wgt, k * wgt


def _block_kernel(x_ref, g1_ref, win_ref, pw_ref, ps_ref, lbl_ref, rng_ref, wout_ref, gf_ref,
                  tri_ref, lvl_ref, o_ref, hist_ref, st_ref, y_ref):
    tj = pl.program_id(1)
    tt = TILE
    d = D_MODEL
    all_k = slice(0, d)

    @pl.when(tj == 0)
    def _():
        hist_ref[0:HIST, :] = jnp.zeros((HIST, d), F32)
        st_ref[...] = jnp.zeros_like(st_ref)

    x = x_ref[0]
    ms = jnp.mean(x * x, axis=-1, keepdims=True)
    hb = ((x * lax.rsqrt(ms + EPS)) * g1_ref[...]).astype(BF16)

    def proj(k):
        return _dot(hb, _unpack_rows(win_ref, all_k, slice(k * d, (k + 1) * d)))

    hist_ref[HIST:HIST + tt, :] = proj(0)
    pgate = proj(1)
    pos = tj * tt + lax.broadcasted_iota(jnp.int32, (tt, 1), 0)
    for g, w in enumerate(POOL_WINDOWS):
        cs = slice(g * POOL_GROUP, (g + 1) * POOL_GROUP)
        ext = hist_ref[:, cs]
        acc = ext
        span = 1
        while span < w:
            acc = acc + pltpu.roll(acc, span, 0)
            span *= 2
        ug = ext[HIST:HIST + tt]
        cnt = jnp.minimum(pos + 1, w).astype(F32)
        pm = acc[HIST:HIST + tt] / cnt - ug
        mixed = _dot(pm.astype(BF16), pltpu.bitcast(pw_ref[g], BF16))
        y_ref[:, cs] = (mixed * ps_ref[:, cs] * _silu(pgate[:, cs])).astype(BF16)
    hist_ref[0:HIST, :] = hist_ref[tt:tt + HIST, :]

    lbl = lbl_ref[...]
    e = jnp.exp(lbl - jnp.max(lbl, axis=0, keepdims=True))
    lb = e[0:1, :] / jnp.sum(e, axis=0, keepdims=True)

    f = lb + (1.0 - lb) * _sigmoid(proj(3))
    lf2 = jnp.log2(f)
    kk = 1.0 - f
    qs = _silu(proj(2))
    v = proj(4)
    rg = proj(5)

    h1 = lf2.astype(BF16)
    r1 = lf2 - h1.astype(F32)
    h2 = r1.astype(BF16)
    h3 = (r1 - h2.astype(F32)).astype(BF16)
    tri = tri_ref[...]
    lvl = lvl_ref[...]
    for c in range(tt // CHUNK):
        rows = slice(c * CHUNK, (c + 1) * CHUNK)
        gcum = _dot(tri, h1[rows]) + _dot(tri, h2[rows]) + _dot(tri, h3[rows])
        for h in range(N_HEADS):
            hs = slice(h * HEAD_DIM, (h + 1) * HEAD_DIM)
            g2 = gcum[:, hs]
            qh = qs[rows, hs]
            kh = kk[rows, hs]
            vh = v[rows, hs].astype(BF16)
            a = jnp.where(lvl == N_LEVELS, jnp.sum(qh * kh, axis=-1, keepdims=True), 0.0)
            for l in range(N_LEVELS):
                ql, kl = _level_operands(qh, kh, f[rows, hs], g2, l)
                a = jnp.where(lvl == l, _nt_dot(ql.astype(BF16), kl.astype(BF16)), a)
            st = st_ref[h]
            qhat = (qh * jnp.exp2(g2)).astype(BF16)
            o = _dot(a.astype(BF16), vh) + _nt_dot(qhat, st.astype(BF16))
            gl = g2[CHUNK - 1:CHUNK, :]
            kdec = (kh * jnp.exp2(gl - g2)).astype(BF16)
            st_ref[h] = st * jnp.exp2(gl) + _tn_dot(vh, kdec)
            msr = jnp.mean(o * o, axis=-1, keepdims=True)
            rec = (o * lax.rsqrt(msr + EPS)) * rng_ref[:, hs]
            y_ref[rows, d + h * HEAD_DIM:d + (h + 1) * HEAD_DIM] = (rec * _silu(rg[rows, hs])).astype(BF16)

    yo = _dot(y_ref[...], _unpack_rows(wout_ref, slice(0, 2 * d), all_k)) + x
    ms2 = jnp.mean(yo * yo, axis=-1, keepdims=True)
    o_ref[0] = (yo * lax.rsqrt(ms2 + EPS)) * gf_ref[...]


def _level_table(n):
    t = np.arange(n)[:, None]
    s = np.arange(n)[None, :]
    x = t ^ s
    hb = np.floor(np.log2(np.maximum(x, 1))).astype(np.int32)
    return np.where(s < t, hb, np.where(s == t, N_LEVELS, -1)).astype(np.int32)


def _const_spec(shape):
    zeros = (0,) * len(shape)
    return pl.BlockSpec(shape, lambda b, j: zeros, pipeline_mode=pl.Buffered(1))


@jax.jit
def kernel(x, norm1_g, w_in, pool_w, pool_scale, lb_logits, rec_norm_g, w_out, final_norm_g):
    batch, seq, d = x.shape
    assert d == D_MODEL and seq % TILE == 0 and TILE % CHUNK == 0
    assert w_in.shape == (1, d, N_PROJ * d) and w_out.shape == (1, 2 * d, d)
    tt = TILE
    win = _pack_rows(w_in[0])
    wout = _pack_rows(w_out[0])
    pw = _pack_rows(pool_w[0].reshape(N_GROUPS * POOL_GROUP, POOL_GROUP)).reshape(
        N_GROUPS, POOL_GROUP // ROWS_PER_WORD, POOL_GROUP)
    tri = jnp.asarray(np.tril(np.ones((CHUNK, CHUNK), np.float32)), dtype=BF16)
    lvl = jnp.asarray(_level_table(CHUNK))

    return pl.pallas_call(
        _block_kernel,
        grid=(batch, seq // tt),
        in_specs=[
            pl.BlockSpec((1, tt, d), lambda b, j: (b, j, 0)),
            _const_spec((1, d)),
            _const_spec(win.shape),
            _const_spec(pw.shape),
            _const_spec((1, d)),
            _const_spec(lb_logits.shape),
            _const_spec((1, d)),
            _const_spec(wout.shape),
            _const_spec((1, d)),
            _const_spec((CHUNK, CHUNK)),
            _const_spec((CHUNK, CHUNK)),
        ],
        out_specs=pl.BlockSpec((1, tt, d), lambda b, j: (b, j, 0)),
        out_shape=jax.ShapeDtypeStruct(x.shape, x.dtype),
        scratch_shapes=[
            pltpu.VMEM((HIST + tt, d), F32),
            pltpu.VMEM((N_HEADS, HEAD_DIM, HEAD_DIM), F32),
            pltpu.VMEM((tt, 2 * d), BF16),
        ],
        compiler_params=pltpu.CompilerParams(
            dimension_semantics=("arbitrary", "arbitrary"),
            vmem_limit_bytes=VMEM_LIMIT_BYTES),
        name="hybrid_pool_hgrn2_block",
    )(x, norm1_g, win, pw, pool_scale, lb_logits, rec_norm_g, wout, final_norm_g.reshape(1, d), tri, lvl)
```

```python
import numpy as np
import jax
import jax.numpy as jnp
from jax import lax
from jax.experimental import pallas as pl
from jax.experimental.pallas import tpu as pltpu

D_MODEL = 1024
HEAD_DIM = 128
N_HEADS = D_MODEL // HEAD_DIM
POOL_WINDOWS = (2, 4, 8, 16)
N_GROUPS = len(POOL_WINDOWS)
POOL_GROUP = D_MODEL // N_GROUPS
N_PROJ = 6
EPS = 1e-6
NEG_LOG2E = -1.4426950408889634

CHUNK = 128
N_LEVELS = 7
TILE = 256
HIST = 16
SUBLANES = 8
ROWS_PER_WORD = 2
PACK_BLOCK_ROWS = 256
OUT_PIECE_ROWS = 256
OUT_PIECE_LAG = 2
VMEM_LIMIT_BYTES = 56 * 1024 * 1024

F32 = jnp.float32
BF16 = jnp.bfloat16


def _sigmoid(z):
    return 1.0 / (1.0 + jnp.exp2(z * NEG_LOG2E))


def _silu(z):
    return z * _sigmoid(z)


def _dot(a, b):
    return jnp.dot(a, b, preferred_element_type=F32)


def _nt_dot(a, b):
    return lax.dot_general(a, b, (((1,), (1,)), ((), ())), preferred_element_type=F32)


def _tn_dot(a, b):
    return lax.dot_general(a, b, (((0,), (0,)), ((), ())), preferred_element_type=F32)


def _pack_kernel(w_ref, o_ref):
    o_ref[...] = pltpu.bitcast(w_ref[...].astype(BF16), jnp.uint32)


def _pack_rows(w):
    k, n = w.shape
    assert k % PACK_BLOCK_ROWS == 0
    return pl.pallas_call(
        _pack_kernel,
        grid=(k // PACK_BLOCK_ROWS,),
        in_specs=[pl.BlockSpec((PACK_BLOCK_ROWS, n), lambda i: (i, 0))],
        out_specs=pl.BlockSpec((PACK_BLOCK_ROWS // ROWS_PER_WORD, n), lambda i: (i, 0)),
        out_shape=jax.ShapeDtypeStruct((k // ROWS_PER_WORD, n), jnp.uint32),
        name="pack_weight_rows",
    )(w)


def _unpack_rows(w_ref, rows, cols):
    assert rows.start % ROWS_PER_WORD == 0 and rows.stop % ROWS_PER_WORD == 0
    return pltpu.bitcast(w_ref[rows.start // ROWS_PER_WORD:rows.stop // ROWS_PER_WORD, cols], BF16)


def _level_operands(q, k, f, g2, level):
    m = 1 << level
    blk = 2 * m
    rows, lanes = g2.shape
    if level == 0:
        return q * f, k
    if blk > SUBLANES:
        qs, ks = [], []
        for b in range(rows // blk):
            lo = slice(b * blk, b * blk + m)
            hi = slice(b * blk + m, (b + 1) * blk)
            mid = g2[b * blk + m - 1:b * blk + m, :]
            qs += [jnp.zeros((m, lanes), F32), q[hi] * jnp.exp2(g2[hi] - mid)]
            ks += [k[lo] * jnp.exp2(mid - g2[lo]), jnp.zeros((m, lanes), F32)]
        return jnp.concatenate(qs, axis=0), jnp.concatenate(ks, axis=0)
    g3 = g2.reshape(rows // SUBLANES, SUBLANES, lanes)
    sub = lax.broadcasted_iota(jnp.int32, g3.shape, 1)
    mid = None
    for jb in range(SUBLANES // blk):
        row = jb * blk + m - 1
        bc = jnp.broadcast_to(g3[:, row:row + 1, :], g3.shape)
        mid = bc if mid is None else jnp.where(sub >= jb * blk, bc, mid)
    wgt = jnp.exp2(-jnp.abs(g3 - mid)).reshape(rows, lanes)
    return q * wgt, k * wgt


def _block_kernel(x_ref, g1_ref, win_ref, pw_ref, ps_ref, lbl_ref, rng_ref, wout_ref, gf_ref,
                  tri_ref, lvl_ref, o_ref, hist_ref, st_ref, y_ref):
    tj = pl.program_id(1)
    tt = TILE
    d = D_MODEL
    all_k = slice(0, d)

    @pl.when(tj == 0)
    def _():
        hist_ref[0:HIST, :] = jnp.zeros((HIST, d), F32)
        st_ref[...] = jnp.zeros_like(st_ref)

    x = x_ref[0]
    ms = jnp.mean(x * x, axis=-1, keepdims=True)
    hb = ((x * lax.rsqrt(ms + EPS)) * g1_ref[...]).astype(BF16)

    def proj(k):
        return _dot(hb, _unpack_rows(win_ref, all_k, slice(k * d, (k + 1) * d)))

    hist_ref[HIST:HIST + tt, :] = proj(0)
    pgate = proj(1)
    pos = tj * tt + lax.broadcasted_iota(jnp.int32, (tt, 1), 0)
    for g, w in enumerate(POOL_WINDOWS):
        cs = slice(g * POOL_GROUP, (g + 1) * POOL_GROUP)
        ext = hist_ref[:, cs]
        acc = ext
        span = 1
        while span < w:
            acc = acc + pltpu.roll(acc, span, 0)
            span *= 2
        ug = ext[HIST:HIST + tt]
        cnt = jnp.minimum(pos + 1, w).astype(F32)
        pm = acc[HIST:HIST + tt] / cnt - ug
        mixed = _dot(pm.astype(BF16), pltpu.bitcast(pw_ref[g], BF16))
        y_ref[:, cs] = (mixed * ps_ref[:, cs] * _silu(pgate[:, cs])).astype(BF16)
    hist_ref[0:HIST, :] = hist_ref[tt:tt + HIST, :]

    lbl = lbl_ref[...]
    e = jnp.exp(lbl - jnp.max(lbl, axis=0, keepdims=True))
    lb = e[0:1, :] / jnp.sum(e, axis=0, keepdims=True)

    f = lb + (1.0 - lb) * _sigmoid(proj(3))
    lf2 = jnp.log2(f)
    kk = 1.0 - f
    qs = _silu(proj(2))
    v = proj(4)
    rg = proj(5)

    h1 = lf2.astype(BF16)
    r1 = lf2 - h1.astype(F32)
    h2 = r1.astype(BF16)
    h3 = (r1 - h2.astype(F32)).astype(BF16)
    tri = tri_ref[...]
    lvl = lvl_ref[...]
    gcum = []
    for c in range(tt // CHUNK):
        rows = slice(c * CHUNK, (c + 1) * CHUNK)
        gcum.append(_dot(tri, h1[rows]) + _dot(tri, h2[rows]) + _dot(tri, h3[rows]))

    def scores(c, h):
        rows = slice(c * CHUNK, (c + 1) * CHUNK)
        hs = slice(h * HEAD_DIM, (h + 1) * HEAD_DIM)
        g2 = gcum[c][:, hs]
        qh = qs[rows, hs]
        kh = kk[rows, hs]
        levels = []
        for l in range(N_LEVELS):
            ql, kl = _level_operands(qh, kh, f[rows, hs], g2, l)
            levels.append(_nt_dot(ql.astype(BF16), kl.astype(BF16)))
        gl = g2[CHUNK - 1:CHUNK, :]
        return dict(levels=levels, diag=jnp.sum(qh * kh, axis=-1, keepdims=True),
                    vh=v[rows, hs].astype(BF16), qhat=(qh * jnp.exp2(g2)).astype(BF16),
                    kdec=(kh * jnp.exp2(gl - g2)).astype(BF16), decay=jnp.exp2(gl))

    def readout(c, h, sc):
        rows = slice(c * CHUNK, (c + 1) * CHUNK)
        hs = slice(h * HEAD_DIM, (h + 1) * HEAD_DIM)
        a = jnp.where(lvl == N_LEVELS, sc["diag"], 0.0)
        for l in range(N_LEVELS):
            a = jnp.where(lvl == l, sc["levels"][l], a)
        st = st_ref[h]
        o = _dot(a.astype(BF16), sc["vh"]) + _nt_dot(sc["qhat"], st.astype(BF16))
        st_ref[h] = st * sc["decay"] + _tn_dot(sc["vh"], sc["kdec"])
        msr = jnp.mean(o * o, axis=-1, keepdims=True)
        rec = (o * lax.rsqrt(msr + EPS)) * rng_ref[:, hs]
        y_ref[rows, d + h * HEAD_DIM:d + (h + 1) * HEAD_DIM] = (rec * _silu(rg[rows, hs])).astype(BF16)

    def out_piece(piece):
        ks = slice(piece * OUT_PIECE_ROWS, (piece + 1) * OUT_PIECE_ROWS)
        return _dot(y_ref[:, ks], _unpack_rows(wout_ref, ks, all_k))

    order = [(c, h) for c in range(tt // CHUNK) for h in range(N_HEADS)]
    n_pool_pieces = d // OUT_PIECE_ROWS
    heads_per_piece = OUT_PIECE_ROWS // HEAD_DIM
    yo = x
    sc_next = scores(*order[0])
    for i, (c, h) in enumerate(order):
        sc = sc_next
        if i + 1 < len(order):
            sc_next = scores(*order[i + 1])
        readout(c, h, sc)
        if c == 0 and h % heads_per_piece == heads_per_piece - 1:
            yo = yo + out_piece(h // heads_per_piece)
        if c == tt // CHUNK - 1:
            done = h - OUT_PIECE_LAG
            if done >= 0 and done % heads_per_piece == heads_per_piece - 1:
                yo = yo + out_piece(n_pool_pieces + done // heads_per_piece)
    for done in range(N_HEADS - OUT_PIECE_LAG, N_HEADS):
        if done % heads_per_piece == heads_per_piece - 1:
            yo = yo + out_piece(n_pool_pieces + done // heads_per_piece)
    assert tt // CHUNK >= 2 and n_pool_pieces == N_HEADS // heads_per_piece

    ms2 = jnp.mean(yo * yo, axis=-1, keepdims=True)
    o_ref[0] = (yo * lax.rsqrt(ms2 + EPS)) * gf_ref[...]


def _level_table(n):
    t = np.arange(n)[:, None]
    s = np.arange(n)[None, :]
    x = t ^ s
    hb = np.floor(np.log2(np.maximum(x, 1))).astype(np.int32)
    return np.where(s < t, hb, np.where(s == t, N_LEVELS, -1)).astype(np.int32)


def _const_spec(shape):
    zeros = (0,) * len(shape)
    return pl.BlockSpec(shape, lambda b, j: zeros, pipeline_mode=pl.Buffered(1))


@jax.jit
def kernel(x, norm1_g, w_in, pool_w, pool_scale, lb_logits, rec_norm_g, w_out, final_norm_g):
    batch, seq, d = x.shape
    assert d == D_MODEL and seq % TILE == 0 and TILE % CHUNK == 0
    assert w_in.shape == (1, d, N_PROJ * d) and w_out.shape == (1, 2 * d, d)
    tt = TILE
    win = _pack_rows(w_in[0])
    wout = _pack_rows(w_out[0])
    pw = _pack_rows(pool_w[0].reshape(N_GROUPS * POOL_GROUP, POOL_GROUP)).reshape(
        N_GROUPS, POOL_GROUP // ROWS_PER_WORD, POOL_GROUP)
    tri = jnp.asarray(np.tril(np.ones((CHUNK, CHUNK), np.float32)), dtype=BF16)
    lvl = jnp.asarray(_level_table(CHUNK))

    return pl.pallas_call(
        _block_kernel,
        grid=(batch, seq // tt),
        in_specs=[
            pl.BlockSpec((1, tt, d), lambda b, j: (b, j, 0)),
            _const_spec((1, d)),
            _const_spec(win.shape),
            _const_spec(pw.shape),
            _const_spec((1, d)),
            _const_spec(lb_logits.shape),
            _const_spec((1, d)),
            _const_spec(wout.shape),
            _const_spec((1, d)),
            _const_spec((CHUNK, CHUNK)),
            _const_spec((CHUNK, CHUNK)),
        ],
        out_specs=pl.BlockSpec((1, tt, d), lambda b, j: (b, j, 0)),
        out_shape=jax.ShapeDtypeStruct(x.shape, x.dtype),
        scratch_shapes=[
            pltpu.VMEM((HIST + tt, d), F32),
            pltpu.VMEM((N_HEADS, HEAD_DIM, HEAD_DIM), F32),
            pltpu.VMEM((tt, 2 * d), BF16),
        ],
        compiler_params=pltpu.CompilerParams(
            dimension_semantics=("arbitrary", "arbitrary"),
            vmem_limit_bytes=VMEM_LIMIT_BYTES),
        name="hybrid_pool_hgrn2_block",
    )(x, norm1_g, win, pw, pool_scale, lb_logits, rec_norm_g, wout, final_norm_g.reshape(1, d), tri, lvl)
```

```python
import numpy as np
import jax
import jax.numpy as jnp
from jax import lax
from jax.experimental import pallas as pl
from jax.experimental.pallas import tpu as pltpu

D_MODEL = 1024
HEAD_DIM = 128
N_HEADS = D_MODEL // HEAD_DIM
POOL_WINDOWS = (2, 4, 8, 16)
N_GROUPS = len(POOL_WINDOWS)
POOL_GROUP = D_MODEL // N_GROUPS
N_PROJ = 6
EPS = 1e-6
NEG_LOG2E = -1.4426950408889634

CHUNK = 128
N_LEVELS = 7
TILE = 256
HIST = 16
SUBLANES = 8
ROWS_PER_WORD = 2
PACK_BLOCK_ROWS = 256
OUT_PIECE_ROWS = 256
OUT_PIECE_LAG = 2
VMEM_LIMIT_BYTES = 56 * 1024 * 1024

F32 = jnp.float32
BF16 = jnp.bfloat16


def _sigmoid(z):
    return 1.0 / (1.0 + jnp.exp2(z * NEG_LOG2E))


def _silu(z):
    return z * _sigmoid(z)


def _dot(a, b):
    return jnp.dot(a, b, preferred_element_type=F32)


def _nt_dot(a, b):
    return lax.dot_general(a, b, (((1,), (1,)), ((), ())), preferred_element_type=F32)


def _tn_dot(a, b):
    return lax.dot_general(a, b, (((0,), (0,)), ((), ())), preferred_element_type=F32)


def _pack_kernel(w_ref, o_ref):
    o_ref[...] = pltpu.bitcast(w_ref[...].astype(BF16), jnp.uint32)


def _pack_rows(w):
    k, n = w.shape
    assert k % PACK_BLOCK_ROWS == 0
    return pl.pallas_call(
        _pack_kernel,
        grid=(k // PACK_BLOCK_ROWS,),
        in_specs=[pl.BlockSpec((PACK_BLOCK_ROWS, n), lambda i: (i, 0))],
        out_specs=pl.BlockSpec((PACK_BLOCK_ROWS // ROWS_PER_WORD, n), lambda i: (i, 0)),
        out_shape=jax.ShapeDtypeStruct((k // ROWS_PER_WORD, n), jnp.uint32),
        name="pack_weight_rows",
    )(w)


def _unpack_rows(w_ref, rows, cols):
    assert rows.start % ROWS_PER_WORD == 0 and rows.stop % ROWS_PER_WORD == 0
    return pltpu.bitcast(w_ref[rows.start // ROWS_PER_WORD:rows.stop // ROWS_PER_WORD, cols], BF16)


def _level_operands(q, k, f, g2, level):
    m = 1 << level
    blk = 2 * m
    rows, lanes = g2.shape
    if level == 0:
        return q * f, k
    if blk > SUBLANES:
        qs, ks = [], []
        for b in range(rows // blk):
            lo = slice(b * blk, b * blk + m)
            hi = slice(b * blk + m, (b + 1) * blk)
            mid = g2[b * blk + m - 1:b * blk + m, :]
            qs += [q[hi] * jnp.exp2(g2[hi] - mid)]
            ks += [k[lo] * jnp.exp2(mid - g2[lo]), jnp.zeros((m, lanes), F32)]
        return jnp.concatenate(qs, axis=0), jnp.concatenate(ks, axis=0)
    g3 = g2.reshape(rows // SUBLANES, SUBLANES, lanes)
    sub = lax.broadcasted_iota(jnp.int32, g3.shape, 1)
    mid = None
    for jb in range(SUBLANES // blk):
        row = jb * blk + m - 1
        bc = jnp.broadcast_to(g3[:, row:row + 1, :], g3.shape)
        mid = bc if mid is None else jnp.where(sub >= jb * blk, bc, mid)
    wgt = jnp.exp2(-jnp.abs(g3 - mid)).reshape(rows, lanes)
    return q * wgt, k * wgt


def _second_half_rows(level, n_rows):
    m = 1 << level
    if 2 * m <= SUBLANES:
        return None
    return [(b * 2 * m + m + j, b * m + j) for b in range(n_rows // (2 * m)) for j in range(0, m, SUBLANES)]


def _block_kernel(x_ref, g1_ref, win_ref, pw_ref, ps_ref, lbl_ref, rng_ref, wout_ref, gf_ref,
                  tri_ref, lvl_ref, o_ref, hist_ref, st_ref, y_ref):
    tj = pl.program_id(1)
    tt = TILE
    d = D_MODEL
    all_k = slice(0, d)

    @pl.when(tj == 0)
    def _():
        hist_ref[0:HIST, :] = jnp.zeros((HIST, d), F32)
        st_ref[...] = jnp.zeros_like(st_ref)

    x = x_ref[0]
    ms = jnp.mean(x * x, axis=-1, keepdims=True)
    hb = ((x * lax.rsqrt(ms + EPS)) * g1_ref[...]).astype(BF16)

    def proj(k):
        return _dot(hb, _unpack_rows(win_ref, all_k, slice(k * d, (k + 1) * d)))

    hist_ref[HIST:HIST + tt, :] = proj(0)
    pgate = proj(1)
    pos = tj * tt + lax.broadcasted_iota(jnp.int32, (tt, 1), 0)
    for g, w in enumerate(POOL_WINDOWS):
        cs = slice(g * POOL_GROUP, (g + 1) * POOL_GROUP)
        ext = hist_ref[:, cs]
        acc = ext
        span = 1
        while span < w:
            acc = acc + pltpu.roll(acc, span, 0)
            span *= 2
        ug = ext[HIST:HIST + tt]
        cnt = jnp.minimum(pos + 1, w).astype(F32)
        pm = acc[HIST:HIST + tt] / cnt - ug
        mixed = _dot(pm.astype(BF16), pltpu.bitcast(pw_ref[g], BF16))
        y_ref[:, cs] = (mixed * ps_ref[:, cs] * _silu(pgate[:, cs])).astype(BF16)
    hist_ref[0:HIST, :] = hist_ref[tt:tt + HIST, :]

    lbl = lbl_ref[...]
    e = jnp.exp(lbl - jnp.max(lbl, axis=0, keepdims=True))
    lb = e[0:1, :] / jnp.sum(e, axis=0, keepdims=True)

    f = lb + (1.0 - lb) * _sigmoid(proj(3))
    lf2 = jnp.log2(f)
    kk = 1.0 - f
    qs = _silu(proj(2))
    v = proj(4)
    rg = proj(5)

    h1 = lf2.astype(BF16)
    h2 = (lf2 - h1.astype(F32)).astype(BF16)
    tri = tri_ref[...]
    lvl = lvl_ref[...]
    gcum = []
    for c in range(tt // CHUNK):
        rows = slice(c * CHUNK, (c + 1) * CHUNK)
        gcum.append(_dot(tri, h1[rows]) + _dot(tri, h2[rows]))

    def scores(c, h):
        rows = slice(c * CHUNK, (c + 1) * CHUNK)
        hs = slice(h * HEAD_DIM, (h + 1) * HEAD_DIM)
        g2 = gcum[c][:, hs]
        qh = qs[rows, hs]
        kh = kk[rows, hs]
        levels = []
        for l in range(N_LEVELS):
            ql, kl = _level_operands(qh, kh, f[rows, hs], g2, l)
            levels.append(_nt_dot(ql.astype(BF16), kl.astype(BF16)))
        gl = g2[CHUNK - 1:CHUNK, :]
        return dict(levels=levels, diag=jnp.sum(qh * kh, axis=-1, keepdims=True),
                    vh=v[rows, hs].astype(BF16), qhat=(qh * jnp.exp2(g2)).astype(BF16),
                    kdec=(kh * jnp.exp2(gl - g2)).astype(BF16), decay=jnp.exp2(gl))

    def readout(c, h, sc):
        rows = slice(c * CHUNK, (c + 1) * CHUNK)
        hs = slice(h * HEAD_DIM, (h + 1) * HEAD_DIM)
        vrows = [slice(r, r + SUBLANES) for r in range(0, CHUNK, SUBLANES)]
        a = [jnp.where(lvl[r] == N_LEVELS, sc["diag"][r], 0.0) for r in vrows]
        for l in range(N_LEVELS):
            lev = sc["levels"][l]
            pairs = _second_half_rows(l, CHUNK) or [(r.start, r.start) for r in vrows]
            for full, compact in pairs:
                i = full // SUBLANES
                a[i] = jnp.where(lvl[vrows[i]] == l, lev[compact:compact + SUBLANES], a[i])
        a = jnp.concatenate(a, axis=0)
        st = st_ref[h]
        o = _dot(a.astype(BF16), sc["vh"]) + _nt_dot(sc["qhat"], st.astype(BF16))
        st_ref[h] = st * sc["decay"] + _tn_dot(sc["vh"], sc["kdec"])
        msr = jnp.mean(o * o, axis=-1, keepdims=True)
        rec = (o * lax.rsqrt(msr + EPS)) * rng_ref[:, hs]
        y_ref[rows, d + h * HEAD_DIM:d + (h + 1) * HEAD_DIM] = (rec * _silu(rg[rows, hs])).astype(BF16)

    def out_piece(piece):
        ks = slice(piece * OUT_PIECE_ROWS, (piece + 1) * OUT_PIECE_ROWS)
        return _dot(y_ref[:, ks], _unpack_rows(wout_ref, ks, all_k))

    order = [(c, h) for c in range(tt // CHUNK) for h in range(N_HEADS)]
    n_pool_pieces = d // OUT_PIECE_ROWS
    heads_per_piece = OUT_PIECE_ROWS // HEAD_DIM
    yo = x
    sc_next = scores(*order[0])
    for i, (c, h) in enumerate(order):
        sc = sc_next
        if i + 1 < len(order):
            sc_next = scores(*order[i + 1])
        readout(c, h, sc)
        if c == 0 and h % heads_per_piece == heads_per_piece - 1:
            yo = yo + out_piece(h // heads_per_piece)
        if c == tt // CHUNK - 1:
            done = h - OUT_PIECE_LAG
            if done >= 0 and done % heads_per_piece == heads_per_piece - 1:
                yo = yo + out_piece(n_pool_pieces + done // heads_per_piece)
    for done in range(N_HEADS - OUT_PIECE_LAG, N_HEADS):
        if done % heads_per_piece == heads_per_piece - 1:
            yo = yo + out_piece(n_pool_pieces + done // heads_per_piece)
    assert tt // CHUNK >= 2 and n_pool_pieces == N_HEADS // heads_per_piece

    ms2 = jnp.mean(yo * yo, axis=-1, keepdims=True)
    o_ref[0] = (yo * lax.rsqrt(ms2 + EPS)) * gf_ref[...]


def _level_table(n):
    t = np.arange(n)[:, None]
    s = np.arange(n)[None, :]
    x = t ^ s
    hb = np.floor(np.log2(np.maximum(x, 1))).astype(np.int32)
    return np.where(s < t, hb, np.where(s == t, N_LEVELS, -1)).astype(np.int32)


def _const_spec(shape):
    zeros = (0,) * len(shape)
    return pl.BlockSpec(shape, lambda b, j: zeros, pipeline_mode=pl.Buffered(1))


@jax.jit
def kernel(x, norm1_g, w_in, pool_w, pool_scale, lb_logits, rec_norm_g, w_out, final_norm_g):
    batch, seq, d = x.shape
    assert d == D_MODEL and seq % TILE == 0 and TILE % CHUNK == 0
    assert w_in.shape == (1, d, N_PROJ * d) and w_out.shape == (1, 2 * d, d)
    tt = TILE
    win = _pack_rows(w_in[0])
    wout = _pack_rows(w_out[0])
    pw = _pack_rows(pool_w[0].reshape(N_GROUPS * POOL_GROUP, POOL_GROUP)).reshape(
        N_GROUPS, POOL_GROUP // ROWS_PER_WORD, POOL_GROUP)
    tri = jnp.asarray(np.tril(np.ones((CHUNK, CHUNK), np.float32)), dtype=BF16)
    lvl = jnp.asarray(_level_table(CHUNK))

    return pl.pallas_call(
        _block_kernel,
        grid=(batch, seq // tt),
        in_specs=[
            pl.BlockSpec((1, tt, d), lambda b, j: (b, j, 0)),
            _const_spec((1, d)),
            _const_spec(win.shape),
            _const_spec(pw.shape),
            _const_spec((1, d)),
            _const_spec(lb_logits.shape),
            _const_spec((1, d)),
            _const_spec(wout.shape),
            _const_spec((1, d)),
            _const_spec((CHUNK, CHUNK)),
            _const_spec((CHUNK, CHUNK)),
        ],
        out_specs=pl.BlockSpec((1, tt, d), lambda b, j: (b, j, 0)),
        out_shape=jax.ShapeDtypeStruct(x.shape, x.dtype),
        scratch_shapes=[
            pltpu.VMEM((HIST + tt, d), F32),
            pltpu.VMEM((N_HEADS, HEAD_DIM, HEAD_DIM), F32),
            pltpu.VMEM((tt, 2 * d), BF16),
        ],
        compiler_params=pltpu.CompilerParams(
            dimension_semantics=("arbitrary", "arbitrary"),
            vmem_limit_bytes=VMEM_LIMIT_BYTES),
        name="hybrid_pool_hgrn2_block",
    )(x, norm1_g, win, pw, pool_scale, lb_logits, rec_norm_g, wout, final_norm_g.reshape(1, d), tri, lvl)
```

```python
import numpy as np
import jax
import jax.numpy as jnp
from jax import lax
from jax.experimental import pallas as pl
from jax.experimental.pallas import tpu as pltpu

D_MODEL = 1024
HEAD_DIM = 128
N_HEADS = D_MODEL // HEAD_DIM
POOL_WINDOWS = (2, 4, 8, 16)
N_GROUPS = len(POOL_WINDOWS)
POOL_GROUP = D_MODEL // N_GROUPS
N_PROJ = 6
EPS = 1e-6
NEG_LOG2E = -1.4426950408889634

CHUNK = 128
N_LEVELS = 7
TILE = 256
HIST = 16
SUBLANES = 8
ROWS_PER_WORD = 2
PACK_BLOCK_BYTES = 6 * 1024 * 1024
PACK_ROW_ALIGN = 16
OUT_PIECE_ROWS = 256
OUT_PIECE_LAG = 2
LIVE_TILE_TEMPORARIES = 16

F32 = jnp.float32
BF16 = jnp.bfloat16


def _sigmoid(z):
    return 1.0 / (1.0 + jnp.exp2(z * NEG_LOG2E))


def _silu(z):
    return z * _sigmoid(z)


def _dot(a, b):
    return jnp.dot(a, b, preferred_element_type=F32)


def _nt_dot(a, b):
    return lax.dot_general(a, b, (((1,), (1,)), ((), ())), preferred_element_type=F32)


def _tn_dot(a, b):
    return lax.dot_general(a, b, (((0,), (0,)), ((), ())), preferred_element_type=F32)


def _pack_kernel(w_ref, o_ref):
    o_ref[...] = pltpu.bitcast(w_ref[...].astype(BF16), jnp.uint32)


def _pack_rows(w):
    k, n = w.shape
    block_rows = k
    while block_rows * n * w.dtype.itemsize > PACK_BLOCK_BYTES and block_rows % (2 * PACK_ROW_ALIGN) == 0:
        block_rows //= 2
    assert k % block_rows == 0 and block_rows % PACK_ROW_ALIGN == 0
    return pl.pallas_call(
        _pack_kernel,
        grid=(k // block_rows,),
        in_specs=[pl.BlockSpec((block_rows, n), lambda i: (i, 0))],
        out_specs=pl.BlockSpec((block_rows // ROWS_PER_WORD, n), lambda i: (i, 0)),
        out_shape=jax.ShapeDtypeStruct((k // ROWS_PER_WORD, n), jnp.uint32),
        name="pack_weight_rows",
    )(w)


def _unpack_rows(w_ref, rows, cols):
    assert rows.start % ROWS_PER_WORD == 0 and rows.stop % ROWS_PER_WORD == 0
    return pltpu.bitcast(w_ref[rows.start // ROWS_PER_WORD:rows.stop // ROWS_PER_WORD, cols], BF16)


def _level_operands(q, k, f, g2, level):
    m = 1 << level
    blk = 2 * m
    rows, lanes = g2.shape
    if level == 0:
        return q * f, k
    if blk > SUBLANES:
        qs, ks = [], []
        for b in range(rows // blk):
            lo = slice(b * blk, b * blk + m)
            hi = slice(b * blk + m, (b + 1) * blk)
            mid = g2[b * blk + m - 1:b * blk + m, :]
            qs += [q[hi] * jnp.exp2(g2[hi] - mid)]
            ks += [k[lo] * jnp.exp2(mid - g2[lo]), jnp.zeros((m, lanes), F32)]
        return jnp.concatenate(qs, axis=0), jnp.concatenate(ks, axis=0)
    g3 = g2.reshape(rows // SUBLANES, SUBLANES, lanes)
    sub = lax.broadcasted_iota(jnp.int32, g3.shape, 1)
    mid = None
    for jb in range(SUBLANES // blk):
        row = jb * blk + m - 1
        bc = jnp.broadcast_to(g3[:, row:row + 1, :], g3.shape)
        mid = bc if mid is None else jnp.where(sub >= jb * blk, bc, mid)
    wgt = jnp.exp2(-jnp.abs(g3 - mid)).reshape(rows, lanes)
    return q * wgt, k * wgt


def _second_half_rows(level, n_rows):
    m = 1 << level
    if 2 * m <= SUBLANES:
        return None
    return [(b * 2 * m + m + j, b * m + j) for b in range(n_rows // (2 * m)) for j in range(0, m, SUBLANES)]


def _block_kernel(x_ref, g1_ref, win_ref, pw_ref, ps_ref, lbl_ref, rng_ref, wout_ref, gf_ref,
                  tri_ref, lvl_ref, o_ref, hist_ref, st_ref, y_ref):
    tj = pl.program_id(1)
    tt = TILE
    d = D_MODEL
    all_k = slice(0, d)

    @pl.when(tj == 0)
    def _():
        hist_ref[0:HIST, :] = jnp.zeros((HIST, d), F32)
        st_ref[...] = jnp.zeros_like(st_ref)

    x = x_ref[0]
    ms = jnp.mean(x * x, axis=-1, keepdims=True)
    hb = ((x * lax.rsqrt(ms + EPS)) * g1_ref[...]).astype(BF16)

    def proj(k):
        return _dot(hb, _unpack_rows(win_ref, all_k, slice(k * d, (k + 1) * d)))

    hist_ref[HIST:HIST + tt, :] = proj(0)
    pgate = proj(1)
    pos = tj * tt + lax.broadcasted_iota(jnp.int32, (tt, 1), 0)
    for g, w in enumerate(POOL_WINDOWS):
        cs = slice(g * POOL_GROUP, (g + 1) * POOL_GROUP)
        ext = hist_ref[:, cs]
        acc = ext
        span = 1
        while span < w:
            acc = acc + pltpu.roll(acc, span, 0)
            span *= 2
        ug = ext[HIST:HIST + tt]
        cnt = jnp.minimum(pos + 1, w).astype(F32)
        pm = acc[HIST:HIST + tt] / cnt - ug
        mixed = _dot(pm.astype(BF16), pltpu.bitcast(pw_ref[g], BF16))
        y_ref[:, cs] = (mixed * ps_ref[:, cs] * _silu(pgate[:, cs])).astype(BF16)
    hist_ref[0:HIST, :] = hist_ref[tt:tt + HIST, :]

    lbl = lbl_ref[...]
    e = jnp.exp(lbl - jnp.max(lbl, axis=0, keepdims=True))
    lb = e[0:1, :] / jnp.sum(e, axis=0, keepdims=True)

    f = lb + (1.0 - lb) * _sigmoid(proj(3))
    lf2 = jnp.log2(f)
    kk = 1.0 - f
    qs = _silu(proj(2))
    v = proj(4)
    rg = proj(5)

    h1 = lf2.astype(BF16)
    h2 = (lf2 - h1.astype(F32)).astype(BF16)
    tri = tri_ref[...]
    lvl = lvl_ref[...]
    gcum = []
    for c in range(tt // CHUNK):
        rows = slice(c * CHUNK, (c + 1) * CHUNK)
        gcum.append(_dot(tri, h1[rows]) + _dot(tri, h2[rows]))

    def scores(c, h):
        rows = slice(c * CHUNK, (c + 1) * CHUNK)
        hs = slice(h * HEAD_DIM, (h + 1) * HEAD_DIM)
        g2 = gcum[c][:, hs]
        qh = qs[rows, hs]
        kh = kk[rows, hs]
        levels = []
        for l in range(N_LEVELS):
            ql, kl = _level_operands(qh, kh, f[rows, hs], g2, l)
            levels.append(_dot(ql.astype(BF16), kl.T.astype(BF16)))
        gl = g2[CHUNK - 1:CHUNK, :]
        return dict(levels=levels, diag=jnp.sum(qh * kh, axis=-1, keepdims=True),
                    vh=v[rows, hs].astype(BF16), qhat=(qh * jnp.exp2(g2)).astype(BF16),
                    kdec=(kh * jnp.exp2(gl - g2)).astype(BF16), decay=jnp.exp2(gl))

    def readout(c, h, sc):
        rows = slice(c * CHUNK, (c + 1) * CHUNK)
        hs = slice(h * HEAD_DIM, (h + 1) * HEAD_DIM)
        vrows = [slice(r, r + SUBLANES) for r in range(0, CHUNK, SUBLANES)]
        a = [jnp.where(lvl[r] == N_LEVELS, sc["diag"][r], 0.0) for r in vrows]
        for l in range(N_LEVELS):
            lev = sc["levels"][l]
            pairs = _second_half_rows(l, CHUNK) or [(r.start, r.start) for r in vrows]
            for full, compact in pairs:
                i = full // SUBLANES
                a[i] = jnp.where(lvl[vrows[i]] == l, lev[compact:compact + SUBLANES], a[i])
        a = jnp.concatenate(a, axis=0)
        st = st_ref[h]
        o = _dot(a.astype(BF16), sc["vh"]) + _nt_dot(sc["qhat"], st.astype(BF16))
        st_ref[h] = st * sc["decay"] + _tn_dot(sc["vh"], sc["kdec"])
        msr = jnp.mean(o * o, axis=-1, keepdims=True)
        rec = (o * lax.rsqrt(msr + EPS)) * rng_ref[:, hs]
        y_ref[rows, d + h * HEAD_DIM:d + (h + 1) * HEAD_DIM] = (rec * _silu(rg[rows, hs])).astype(BF16)

    def out_piece(piece):
        ks = slice(piece * OUT_PIECE_ROWS, (piece + 1) * OUT_PIECE_ROWS)
        return _dot(y_ref[:, ks], _unpack_rows(wout_ref, ks, all_k))

    order = [(c, h) for c in range(tt // CHUNK) for h in range(N_HEADS)]
    n_pool_pieces = d // OUT_PIECE_ROWS
    heads_per_piece = OUT_PIECE_ROWS // HEAD_DIM
    yo = x
    sc_next = scores(*order[0])
    for i, (c, h) in enumerate(order):
        sc = sc_next
        if i + 1 < len(order):
            sc_next = scores(*order[i + 1])
        readout(c, h, sc)
        if c == 0 and h % heads_per_piece == heads_per_piece - 1:
            yo = yo + out_piece(h // heads_per_piece)
        if c == tt // CHUNK - 1:
            done = h - OUT_PIECE_LAG
            if done >= 0 and done % heads_per_piece == heads_per_piece - 1:
                yo = yo + out_piece(n_pool_pieces + done // heads_per_piece)
    for done in range(N_HEADS - OUT_PIECE_LAG, N_HEADS):
        if done % heads_per_piece == heads_per_piece - 1:
            yo = yo + out_piece(n_pool_pieces + done // heads_per_piece)
    assert tt // CHUNK >= 2 and n_pool_pieces == N_HEADS // heads_per_piece

    ms2 = jnp.mean(yo * yo, axis=-1, keepdims=True)
    o_ref[0] = (yo * lax.rsqrt(ms2 + EPS)) * gf_ref[...]


def _level_table(n):
    t = np.arange(n)[:, None]
    s = np.arange(n)[None, :]
    x = t ^ s
    hb = np.floor(np.log2(np.maximum(x, 1))).astype(np.int32)
    return np.where(s < t, hb, np.where(s == t, N_LEVELS, -1)).astype(np.int32)


def _vmem_limit_bytes(resident_operands, scratch_shapes):
    tile_bytes = TILE * D_MODEL * 4
    resident = sum(int(np.prod(a.shape)) * a.dtype.itemsize for a in resident_operands)
    scratch = sum(int(np.prod(s.shape)) * jnp.dtype(s.dtype).itemsize for s in scratch_shapes)
    return resident + 2 * 2 * tile_bytes + scratch + LIVE_TILE_TEMPORARIES * tile_bytes


def _const_spec(shape):
    zeros = (0,) * len(shape)
    return pl.BlockSpec(shape, lambda b, j: zeros, pipeline_mode=pl.Buffered(1))


@jax.jit
def kernel(x, norm1_g, w_in, pool_w, pool_scale, lb_logits, rec_norm_g, w_out, final_norm_g):
    batch, seq, d = x.shape
    assert d == D_MODEL and seq % TILE == 0 and TILE % CHUNK == 0
    assert w_in.shape == (1, d, N_PROJ * d) and w_out.shape == (1, 2 * d, d)
    tt = TILE
    win = _pack_rows(w_in[0])
    wout = _pack_rows(w_out[0])
    pw = _pack_rows(pool_w[0].reshape(N_GROUPS * POOL_GROUP, POOL_GROUP)).reshape(
        N_GROUPS, POOL_GROUP // ROWS_PER_WORD, POOL_GROUP)
    tri = jnp.asarray(np.tril(np.ones((CHUNK, CHUNK), np.float32)), dtype=BF16)
    lvl = jnp.asarray(_level_table(CHUNK))
    scratch_shapes = [
        pltpu.VMEM((HIST + tt, d), F32),
        pltpu.VMEM((N_HEADS, HEAD_DIM, HEAD_DIM), F32),
        pltpu.VMEM((tt, 2 * d), BF16),
    ]

    return pl.pallas_call(
        _block_kernel,
        grid=(batch, seq // tt),
        in_specs=[
            pl.BlockSpec((1, tt, d), lambda b, j: (b, j, 0)),
            _const_spec((1, d)),
            _const_spec(win.shape),
            _const_spec(pw.shape),
            _const_spec((1, d)),
            _const_spec(lb_logits.shape),
            _const_spec((1, d)),
            _const_spec(wout.shape),
            _const_spec((1, d)),
            _const_spec((CHUNK, CHUNK)),
            _const_spec((CHUNK, CHUNK)),
        ],
        out_specs=pl.BlockSpec((1, tt, d), lambda b, j: (b, j, 0)),
        out_shape=jax.ShapeDtypeStruct(x.shape, x.dtype),
        scratch_shapes=scratch_shapes,
        compiler_params=pltpu.CompilerParams(
            dimension_semantics=("arbitrary", "arbitrary"),
            vmem_limit_bytes=_vmem_limit_bytes((win, wout, pw, tri, lvl), scratch_shapes)),
        name="hybrid_pool_hgrn2_block",
    )(x, norm1_g, win, pw, pool_scale, lb_logits, rec_norm_g, wout, final_norm_g.reshape(1, d), tri, lvl)
```

```python
import numpy as np
import jax
import jax.numpy as jnp
from jax import lax
from jax.experimental import pallas as pl
from jax.experimental.pallas import tpu as pltpu

D_MODEL = 1024
HEAD_DIM = 128
N_HEADS = D_MODEL // HEAD_DIM
POOL_WINDOWS = (2, 4, 8, 16)
N_GROUPS = len(POOL_WINDOWS)
POOL_GROUP = D_MODEL // N_GROUPS
N_PROJ = 6
EPS = 1e-6
NEG_LOG2E = -1.4426950408889634

CHUNK = 128
N_LEVELS = 7
TILE = 256
HIST = 16
SUBLANES = 8
WIN_STAGE_ROWS = 128
WOUT_STAGE_ROWS = 512
OUT_PIECE_ROWS = 256
OUT_PIECE_LAG = 2
LIVE_TILE_TEMPORARIES = 16

F32 = jnp.float32
BF16 = jnp.bfloat16


def _sigmoid(z):
    return 1.0 / (1.0 + jnp.exp2(z * NEG_LOG2E))


def _silu(z):
    return z * _sigmoid(z)


def _dot(a, b):
    return jnp.dot(a, b, preferred_element_type=F32)


def _nt_dot(a, b):
    return lax.dot_general(a, b, (((1,), (1,)), ((), ())), preferred_element_type=F32)


def _tn_dot(a, b):
    return lax.dot_general(a, b, (((0,), (0,)), ((), ())), preferred_element_type=F32)


def _load_as_bf16(src_hbm, dst_ref, stage_ref, sems, chunk_rows):
    rows = src_hbm.shape[0]
    assert rows % chunk_rows == 0 and stage_ref.shape[1:] == (chunk_rows, src_hbm.shape[1])
    n = rows // chunk_rows

    def chunk_copy(i):
        return pltpu.make_async_copy(src_hbm.at[pl.ds(i * chunk_rows, chunk_rows)], stage_ref.at[i % 2],
                                     sems.at[i % 2])

    chunk_copy(0).start()
    for i in range(n):
        if i + 1 < n:
            chunk_copy(i + 1).start()
        chunk_copy(i).wait()
        dst_ref[i * chunk_rows:(i + 1) * chunk_rows, :] = stage_ref[i % 2].astype(BF16)


def _level_operands(q, k, f, g2, level):
    m = 1 << level
    blk = 2 * m
    rows, lanes = g2.shape
    if level == 0:
        return q * f, k
    if blk > SUBLANES:
        qs, ks = [], []
        for b in range(rows // blk):
            lo = slice(b * blk, b * blk + m)
            hi = slice(b * blk + m, (b + 1) * blk)
            mid = g2[b * blk + m - 1:b * blk + m, :]
            qs += [q[hi] * jnp.exp2(g2[hi] - mid)]
            ks += [k[lo] * jnp.exp2(mid - g2[lo]), jnp.zeros((m, lanes), F32)]
        return jnp.concatenate(qs, axis=0), jnp.concatenate(ks, axis=0)
    g3 = g2.reshape(rows // SUBLANES, SUBLANES, lanes)
    sub = lax.broadcasted_iota(jnp.int32, g3.shape, 1)
    mid = None
    for jb in range(SUBLANES // blk):
        row = jb * blk + m - 1
        bc = jnp.broadcast_to(g3[:, row:row + 1, :], g3.shape)
        mid = bc if mid is None else jnp.where(sub >= jb * blk, bc, mid)
    wgt = jnp.exp2(-jnp.abs(g3 - mid)).reshape(rows, lanes)
    return q * wgt, k * wgt


def _second_half_rows(level, n_rows):
    m = 1 << level
    if 2 * m <= SUBLANES:
        return None
    return [(b * 2 * m + m + j, b * m + j) for b in range(n_rows // (2 * m)) for j in range(0, m, SUBLANES)]


def _block_kernel(x_ref, g1_ref, win_hbm, pw_hbm, ps_ref, lbl_ref, rng_ref, wout_hbm, gf_ref,
                  tri_ref, lvl_ref, o_ref, hist_ref, st_ref, y_ref,
                  win_ref, wout_ref, pw_ref, win_stage, wout_stage, pw_stage, sems):
    tj = pl.program_id(1)
    tt = TILE
    d = D_MODEL

    @pl.when((pl.program_id(0) == 0) & (tj == 0))
    def _():
        _load_as_bf16(win_hbm, win_ref, win_stage, sems, WIN_STAGE_ROWS)
        _load_as_bf16(wout_hbm, wout_ref, wout_stage, sems, WOUT_STAGE_ROWS)
        _load_as_bf16(pw_hbm, pw_ref, pw_stage, sems, pw_hbm.shape[0] // 2)

    @pl.when(tj == 0)
    def _():
        hist_ref[0:HIST, :] = jnp.zeros((HIST, d), F32)
        st_ref[...] = jnp.zeros_like(st_ref)

    x = x_ref[0]
    ms = jnp.mean(x * x, axis=-1, keepdims=True)
    hb = ((x * lax.rsqrt(ms + EPS)) * g1_ref[...]).astype(BF16)

    def proj(k):
        return _dot(hb, win_ref[:, k * d:(k + 1) * d])

    hist_ref[HIST:HIST + tt, :] = proj(0)
    pgate = proj(1)
    pos = tj * tt + lax.broadcasted_iota(jnp.int32, (tt, 1), 0)
    for g, w in enumerate(POOL_WINDOWS):
        cs = slice(g * POOL_GROUP, (g + 1) * POOL_GROUP)
        ext = hist_ref[:, cs]
        acc = ext
        span = 1
        while span < w:
            acc = acc + pltpu.roll(acc, span, 0)
            span *= 2
        ug = ext[HIST:HIST + tt]
        cnt = jnp.minimum(pos + 1, w).astype(F32)
        pm = acc[HIST:HIST + tt] / cnt - ug
        mixed = _dot(pm.astype(BF16), pw_ref[g * POOL_GROUP:(g + 1) * POOL_GROUP, :])
        y_ref[:, cs] = (mixed * ps_ref[:, cs] * _silu(pgate[:, cs])).astype(BF16)
    hist_ref[0:HIST, :] = hist_ref[tt:tt + HIST, :]

    lbl = lbl_ref[...]
    e = jnp.exp(lbl - jnp.max(lbl, axis=0, keepdims=True))
    lb = e[0:1, :] / jnp.sum(e, axis=0, keepdims=True)

    f = lb + (1.0 - lb) * _sigmoid(proj(3))
    lf2 = jnp.log2(f)
    kk = 1.0 - f
    qs = _silu(proj(2))
    v = proj(4)
    rg = proj(5)

    h1 = lf2.astype(BF16)
    h2 = (lf2 - h1.astype(F32)).astype(BF16)
    tri = tri_ref[...]
    lvl = lvl_ref[...]
    gcum = []
    for c in range(tt // CHUNK):
        rows = slice(c * CHUNK, (c + 1) * CHUNK)
        gcum.append(_dot(tri, h1[rows]) + _dot(tri, h2[rows]))

    def scores(c, h):
        rows = slice(c * CHUNK, (c + 1) * CHUNK)
        hs = slice(h * HEAD_DIM, (h + 1) * HEAD_DIM)
        g2 = gcum[c][:, hs]
        qh = qs[rows, hs]
        kh = kk[rows, hs]
        levels = []
        for l in range(N_LEVELS):
            ql, kl = _level_operands(qh, kh, f[rows, hs], g2, l)
            levels.append(_dot(ql.astype(BF16), kl.T.astype(BF16)))
        gl = g2[CHUNK - 1:CHUNK, :]
        return dict(levels=levels, diag=jnp.sum(qh * kh, axis=-1, keepdims=True),
                    vh=v[rows, hs].astype(BF16), qhat=(qh * jnp.exp2(g2)).astype(BF16),
                    kdec=(kh * jnp.exp2(gl - g2)).astype(BF16), decay=jnp.exp2(gl))

    def readout(c, h, sc):
        rows = slice(c * CHUNK, (c + 1) * CHUNK)
        hs = slice(h * HEAD_DIM, (h + 1) * HEAD_DIM)
        vrows = [slice(r, r + SUBLANES) for r in range(0, CHUNK, SUBLANES)]
        a = [jnp.where(lvl[r] == N_LEVELS, sc["diag"][r], 0.0) for r in vrows]
        for l in range(N_LEVELS):
            lev = sc["levels"][l]
            pairs = _second_half_rows(l, CHUNK) or [(r.start, r.start) for r in vrows]
            for full, compact in pairs:
                i = full // SUBLANES
                a[i] = jnp.where(lvl[vrows[i]] == l, lev[compact:compact + SUBLANES], a[i])
        a = jnp.concatenate(a, axis=0)
        st = st_ref[h]
        o = _dot(a.astype(BF16), sc["vh"]) + _nt_dot(sc["qhat"], st.astype(BF16))
        st_ref[h] = st * sc["decay"] + _tn_dot(sc["vh"], sc["kdec"])
        msr = jnp.mean(o * o, axis=-1, keepdims=True)
        rec = (o * lax.rsqrt(msr + EPS)) * rng_ref[:, hs]
        y_ref[rows, d + h * HEAD_DIM:d + (h + 1) * HEAD_DIM] = (rec * _silu(rg[rows, hs])).astype(BF16)

    def out_piece(piece):
        ks = slice(piece * OUT_PIECE_ROWS, (piece + 1) * OUT_PIECE_ROWS)
        return _dot(y_ref[:, ks], wout_ref[ks, :])

    order = [(c, h) for c in range(tt // CHUNK) for h in range(N_HEADS)]
    n_pool_pieces = d // OUT_PIECE_ROWS
    heads_per_piece = OUT_PIECE_ROWS // HEAD_DIM
    yo = x
    sc_next = scores(*order[0])
    for i, (c, h) in enumerate(order):
        sc = sc_next
        if i + 1 < len(order):
            sc_next = scores(*order[i + 1])
        readout(c, h, sc)
        if c == 0 and h % heads_per_piece == heads_per_piece - 1:
            yo = yo + out_piece(h // heads_per_piece)
        if c == tt // CHUNK - 1:
            done = h - OUT_PIECE_LAG
            if done >= 0 and done % heads_per_piece == heads_per_piece - 1:
                yo = yo + out_piece(n_pool_pieces + done // heads_per_piece)
    for done in range(N_HEADS - OUT_PIECE_LAG, N_HEADS):
        if done % heads_per_piece == heads_per_piece - 1:
            yo = yo + out_piece(n_pool_pieces + done // heads_per_piece)
    assert tt // CHUNK >= 2 and n_pool_pieces == N_HEADS // heads_per_piece

    ms2 = jnp.mean(yo * yo, axis=-1, keepdims=True)
    o_ref[0] = (yo * lax.rsqrt(ms2 + EPS)) * gf_ref[...]


def _level_table(n):
    t = np.arange(n)[:, None]
    s = np.arange(n)[None, :]
    x = t ^ s
    hb = np.floor(np.log2(np.maximum(x, 1))).astype(np.int32)
    return np.where(s < t, hb, np.where(s == t, N_LEVELS, -1)).astype(np.int32)


def _vmem_limit_bytes(resident_operands, scratch_shapes):
    tile_bytes = TILE * D_MODEL * 4
    resident = sum(int(np.prod(a.shape)) * a.dtype.itemsize for a in resident_operands)
    scratch = sum(int(np.prod(s.shape)) * jnp.dtype(s.dtype).itemsize for s in scratch_shapes)
    return resident + 2 * 2 * tile_bytes + scratch + LIVE_TILE_TEMPORARIES * tile_bytes


def _const_spec(shape):
    zeros = (0,) * len(shape)
    return pl.BlockSpec(shape, lambda b, j: zeros, pipeline_mode=pl.Buffered(1))


@jax.jit
def kernel(x, norm1_g, w_in, pool_w, pool_scale, lb_logits, rec_norm_g, w_out, final_norm_g):
    batch, seq, d = x.shape
    assert d == D_MODEL and seq % TILE == 0 and TILE % CHUNK == 0
    assert w_in.shape == (1, d, N_PROJ * d) and w_out.shape == (1, 2 * d, d)
    tt = TILE
    win = w_in[0]
    wout = w_out[0]
    pw = pool_w[0].reshape(N_GROUPS * POOL_GROUP, POOL_GROUP)
    tri = jnp.asarray(np.tril(np.ones((CHUNK, CHUNK), np.float32)), dtype=BF16)
    lvl = jnp.asarray(_level_table(CHUNK))
    scratch_shapes = [
        pltpu.VMEM((HIST + tt, d), F32),
        pltpu.VMEM((N_HEADS, HEAD_DIM, HEAD_DIM), F32),
        pltpu.VMEM((tt, 2 * d), BF16),
        pltpu.VMEM(win.shape, BF16),
        pltpu.VMEM(wout.shape, BF16),
        pltpu.VMEM(pw.shape, BF16),
        pltpu.VMEM((2, WIN_STAGE_ROWS, win.shape[1]), F32),
        pltpu.VMEM((2, WOUT_STAGE_ROWS, wout.shape[1]), F32),
        pltpu.VMEM((2, pw.shape[0] // 2, pw.shape[1]), F32),
        pltpu.SemaphoreType.DMA((2,)),
    ]
    hbm_spec = pl.BlockSpec(memory_space=pl.ANY)

    return pl.pallas_call(
        _block_kernel,
        grid=(batch, seq // tt),
        in_specs=[
            pl.BlockSpec((1, tt, d), lambda b, j: (b, j, 0)),
            _const_spec((1, d)),
            hbm_spec,
            hbm_spec,
            _const_spec((1, d)),
            _const_spec(lb_logits.shape),
            _const_spec((1, d)),
            hbm_spec,
            _const_spec((1, d)),
            _const_spec((CHUNK, CHUNK)),
            _const_spec((CHUNK, CHUNK)),
        ],
        out_specs=pl.BlockSpec((1, tt, d), lambda b, j: (b, j, 0)),
        out_shape=jax.ShapeDtypeStruct(x.shape, x.dtype),
        scratch_shapes=scratch_shapes,
        compiler_params=pltpu.CompilerParams(
            dimension_semantics=("arbitrary", "arbitrary"),
            vmem_limit_bytes=_vmem_limit_bytes((tri, lvl), scratch_shapes[:-1])),
        name="hybrid_pool_hgrn2_block",
    )(x, norm1_g, win, pw, pool_scale, lb_logits, rec_norm_g, wout, final_norm_g.reshape(1, d), tri, lvl)
```

```python
import functools

import numpy as np
import jax
import jax.numpy as jnp
from jax import lax
from jax.experimental import pallas as pl
from jax.experimental.pallas import tpu as pltpu

D_MODEL = 1024
HEAD_DIM = 128
N_HEADS = D_MODEL // HEAD_DIM
POOL_WINDOWS = (2, 4, 8, 16)
N_GROUPS = len(POOL_WINDOWS)
POOL_GROUP = D_MODEL // N_GROUPS
N_PROJ = 6
EPS = 1e-6
NEG_LOG2E = -1.4426950408889634

CHUNK = 128
N_LEVELS = 7
TILE = 512
HIST = 16
SUBLANES = 8
ROWS_PER_WORD = 2
PACK_BLOCK_BYTES = 6 * 1024 * 1024
PACK_ROW_ALIGN = 16
OUT_PIECE_ROWS = 256
HALF_HEADS = N_HEADS // 2
N_AHEAD = 6
OUT_PIECE_LAG = 2
SEC_F, SEC_Q, SEC_G, SEC_V, SEC_R = range(5)
N_SECTIONS = 5
LIVE_TILE_TEMPORARIES = 11

F32 = jnp.float32
BF16 = jnp.bfloat16


def _sigmoid(z):
    return 1.0 / (1.0 + jnp.exp2(z * NEG_LOG2E))


def _silu(z):
    return z * _sigmoid(z)


def _dot(a, b):
    return jnp.dot(a, b, preferred_element_type=F32)


def _nt_dot(a, b):
    return lax.dot_general(a, b, (((1,), (1,)), ((), ())), preferred_element_type=F32)


def _tn_dot(a, b):
    return lax.dot_general(a, b, (((0,), (0,)), ((), ())), preferred_element_type=F32)


def _pack_kernel(w_ref, o_ref):
    o_ref[...] = pltpu.bitcast(w_ref[...].astype(BF16), jnp.uint32)


def _pack_rows(w):
    k, n = w.shape
    block_rows = k
    while block_rows * n * w.dtype.itemsize > PACK_BLOCK_BYTES and block_rows % (2 * PACK_ROW_ALIGN) == 0:
        block_rows //= 2
    assert k % block_rows == 0 and block_rows % PACK_ROW_ALIGN == 0
    return pl.pallas_call(
        _pack_kernel,
        grid=(k // block_rows,),
        in_specs=[pl.BlockSpec((block_rows, n), lambda i: (i, 0))],
        out_specs=pl.BlockSpec((block_rows // ROWS_PER_WORD, n), lambda i: (i, 0)),
        out_shape=jax.ShapeDtypeStruct((k // ROWS_PER_WORD, n), jnp.uint32),
        name="pack_weight_rows",
    )(w)


def _unpack_rows(w_ref, rows, cols):
    assert rows.start % ROWS_PER_WORD == 0 and rows.stop % ROWS_PER_WORD == 0
    return pltpu.bitcast(w_ref[rows.start // ROWS_PER_WORD:rows.stop // ROWS_PER_WORD, cols], BF16)


def _level_operands(q, k, f, g2, level):
    m = 1 << level
    blk = 2 * m
    rows, lanes = g2.shape
    if level == 0:
        return q * f, k
    if blk > SUBLANES:
        qs, ks = [], []
        for b in range(rows // blk):
            lo = slice(b * blk, b * blk + m)
            hi = slice(b * blk + m, (b + 1) * blk)
            mid = g2[b * blk + m - 1:b * blk + m, :]
            qs += [q[hi] * jnp.exp2(g2[hi] - mid)]
            ks += [k[lo] * jnp.exp2(mid - g2[lo]), jnp.zeros((m, lanes), F32)]
        return jnp.concatenate(qs, axis=0), jnp.concatenate(ks, axis=0)
    g3 = g2.reshape(rows // SUBLANES, SUBLANES, lanes)
    sub = lax.broadcasted_iota(jnp.int32, g3.shape, 1)
    mid = None
    for jb in range(SUBLANES // blk):
        row = jb * blk + m - 1
        bc = jnp.broadcast_to(g3[:, row:row + 1, :], g3.shape)
        mid = bc if mid is None else jnp.where(sub >= jb * blk, bc, mid)
    wgt = jnp.exp2(-jnp.abs(g3 - mid)).reshape(rows, lanes)
    return q * wgt, k * wgt


def _second_half_rows(level, n_rows):
    m = 1 << level
    if 2 * m <= SUBLANES:
        return None
    return [(b * 2 * m + m + j, b * m + j) for b in range(n_rows // (2 * m)) for j in range(0, m, SUBLANES)]


def _block_kernel(x_ref, g1_ref, win_ref, pw_ref, ps_ref, lbl_ref, rng_ref, wout_ref, gf_ref,
                  tri_ref, lvl_ref, o_ref, hist_ref, st_ref, y_ref, sec_ref):
    tj = pl.program_id(1)
    tt = TILE
    d = D_MODEL
    all_k = slice(0, d)

    @pl.when(tj == 0)
    def _():
        hist_ref[0:HIST, :] = jnp.zeros((HIST, d), F32)
        st_ref[...] = jnp.zeros_like(st_ref)

    x = x_ref[0]
    ms = jnp.mean(x * x, axis=-1, keepdims=True)
    hb = ((x * lax.rsqrt(ms + EPS)) * g1_ref[...]).astype(BF16)

    def proj(k):
        return _dot(hb, _unpack_rows(win_ref, all_k, slice(k * d, (k + 1) * d)))

    def proj_half(sec, k, half):
        cols = slice(half * (d // 2), (half + 1) * (d // 2))
        sec_ref[sec, :, cols] = _dot(hb, _unpack_rows(win_ref, all_k, slice(k * d + cols.start, k * d + cols.stop)))

    pool = {}
    pos = tj * tt + lax.broadcasted_iota(jnp.int32, (tt, 1), 0)

    def pool_values():
        hist_ref[HIST:HIST + tt, :] = proj(0)

    def pool_gate():
        pool["gate"] = proj(1)

    def pool_group(g):
        w = POOL_WINDOWS[g]
        cs = slice(g * POOL_GROUP, (g + 1) * POOL_GROUP)
        ext = hist_ref[:, cs]
        acc = ext
        span = 1
        while span < w:
            acc = acc + pltpu.roll(acc, span, 0)
            span *= 2
        ug = ext[HIST:HIST + tt]
        cnt = jnp.minimum(pos + 1, w).astype(F32)
        pm = acc[HIST:HIST + tt] / cnt - ug
        mixed = _dot(pm.astype(BF16), pltpu.bitcast(pw_ref[g], BF16))
        y_ref[:, cs] = (mixed * ps_ref[:, cs] * _silu(pool["gate"][:, cs])).astype(BF16)

    def pool_carry():
        hist_ref[0:HIST, :] = hist_ref[tt:tt + HIST, :]

    pool_tasks = [pool_values, pool_gate] + [functools.partial(pool_group, g) for g in range(N_GROUPS)] + [pool_carry]

    lbl = lbl_ref[...]
    e = jnp.exp(lbl - jnp.max(lbl, axis=0, keepdims=True))
    lb = e[0:1, :] / jnp.sum(e, axis=0, keepdims=True)

    f = lb + (1.0 - lb) * _sigmoid(proj(3))
    lf2 = jnp.log2(f)
    sec_ref[SEC_F] = f
    sec_ref[SEC_Q] = _silu(proj(2))

    h1 = lf2.astype(BF16)
    h2 = (lf2 - h1.astype(F32)).astype(BF16)
    tri = tri_ref[...]
    lvl = lvl_ref[...]
    for c in range(tt // CHUNK):
        rows = slice(c * CHUNK, (c + 1) * CHUNK)
        sec_ref[SEC_G, rows, :] = _dot(tri, h1[rows]) + _dot(tri, h2[rows])

    def scores(c, h):
        rows = slice(c * CHUNK, (c + 1) * CHUNK)
        hs = slice(h * HEAD_DIM, (h + 1) * HEAD_DIM)
        g2 = sec_ref[SEC_G, rows, hs]
        qh = sec_ref[SEC_Q, rows, hs]
        fh = sec_ref[SEC_F, rows, hs]
        kh = 1.0 - fh
        levels = []
        for l in range(N_LEVELS):
            ql, kl = _level_operands(qh, kh, fh, g2, l)
            levels.append(_dot(ql.astype(BF16), kl.T.astype(BF16)))
        gl = g2[CHUNK - 1:CHUNK, :]
        return dict(levels=levels, diag=jnp.sum(qh * kh, axis=-1, keepdims=True),
                    qhat=(qh * jnp.exp2(g2)).astype(BF16),
                    kdec=(kh * jnp.exp2(gl - g2)).astype(BF16), decay=jnp.exp2(gl))

    def readout(c, h, sc):
        rows = slice(c * CHUNK, (c + 1) * CHUNK)
        hs = slice(h * HEAD_DIM, (h + 1) * HEAD_DIM)
        vrows = [slice(r, r + SUBLANES) for r in range(0, CHUNK, SUBLANES)]
        a = [jnp.where(lvl[r] == N_LEVELS, sc["diag"][r], 0.0) for r in vrows]
        for l in range(N_LEVELS):
            lev = sc["levels"][l]
            pairs = _second_half_rows(l, CHUNK) or [(r.start, r.start) for r in vrows]
            for full, compact in pairs:
                i = full // SUBLANES
                a[i] = jnp.where(lvl[vrows[i]] == l, lev[compact:compact + SUBLANES], a[i])
        a = jnp.concatenate(a, axis=0)
        st = st_ref[h]
        vh = sec_ref[SEC_V, rows, hs].astype(BF16)
        o = _dot(a.astype(BF16), vh) + _nt_dot(sc["qhat"], st.astype(BF16))
        st_ref[h] = st * sc["decay"] + _tn_dot(vh, sc["kdec"])
        msr = jnp.mean(o * o, axis=-1, keepdims=True)
        rec = (o * lax.rsqrt(msr + EPS)) * rng_ref[:, hs]
        y_ref[rows, d + h * HEAD_DIM:d + (h + 1) * HEAD_DIM] = (rec * _silu(sec_ref[SEC_R, rows, hs])).astype(BF16)

    def out_piece(piece):
        ks = slice(piece * OUT_PIECE_ROWS, (piece + 1) * OUT_PIECE_ROWS)
        return _dot(y_ref[:, ks], _unpack_rows(wout_ref, ks, all_k))

    order = [(c, h) for c in range(tt // CHUNK) for h in range(N_HEADS)]
    n_pool_pieces = d // OUT_PIECE_ROWS
    heads_per_piece = OUT_PIECE_ROWS // HEAD_DIM
    yo = x
    ready = {}
    for n, task in enumerate(pool_tasks):
        if n < N_AHEAD:
            ready[n] = scores(*order[n])
        task()
    n0 = len(ready)
    assert n0 + 1 < HALF_HEADS * 2
    ready[n0] = scores(*order[n0])
    proj_half(SEC_V, 4, 0)
    ready[n0 + 1] = scores(*order[n0 + 1])
    proj_half(SEC_R, 5, 0)
    for i, (c, h) in enumerate(order):
        if i + 1 < len(order) and i + 1 not in ready:
            ready[i + 1] = scores(*order[i + 1])
        readout(c, h, ready.pop(i))
        if i == 0:
            proj_half(SEC_V, 4, 1)
        if i == HALF_HEADS // 2:
            proj_half(SEC_R, 5, 1)
        if c == 0 and h % heads_per_piece == heads_per_piece - 1:
            yo = yo + out_piece(h // heads_per_piece)
        if c == tt // CHUNK - 1:
            done = h - OUT_PIECE_LAG
            if done >= 0 and done % heads_per_piece == heads_per_piece - 1:
                yo = yo + out_piece(n_pool_pieces + done // heads_per_piece)
    for done in range(N_HEADS - OUT_PIECE_LAG, N_HEADS):
        if done % heads_per_piece == heads_per_piece - 1:
            yo = yo + out_piece(n_pool_pieces + done // heads_per_piece)
    assert tt // CHUNK >= 2 and n_pool_pieces == N_HEADS // heads_per_piece

    ms2 = jnp.mean(yo * yo, axis=-1, keepdims=True)
    o_ref[0] = (yo * lax.rsqrt(ms2 + EPS)) * gf_ref[...]


def _level_table(n):
    t = np.arange(n)[:, None]
    s = np.arange(n)[None, :]
    x = t ^ s
    hb = np.floor(np.log2(np.maximum(x, 1))).astype(np.int32)
    return np.where(s < t, hb, np.where(s == t, N_LEVELS, -1)).astype(np.int32)


def _vmem_limit_bytes(resident_operands, scratch_shapes):
    tile_bytes = TILE * D_MODEL * 4
    resident = sum(int(np.prod(a.shape)) * a.dtype.itemsize for a in resident_operands)
    scratch = sum(int(np.prod(s.shape)) * jnp.dtype(s.dtype).itemsize for s in scratch_shapes)
    return resident + 2 * 2 * tile_bytes + scratch + LIVE_TILE_TEMPORARIES * tile_bytes


def _const_spec(shape):
    zeros = (0,) * len(shape)
    return pl.BlockSpec(shape, lambda b, j: zeros, pipeline_mode=pl.Buffered(1))


@jax.jit
def kernel(x, norm1_g, w_in, pool_w, pool_scale, lb_logits, rec_norm_g, w_out, final_norm_g):
    batch, seq, d = x.shape
    assert d == D_MODEL and seq % TILE == 0 and TILE % CHUNK == 0
    assert w_in.shape == (1, d, N_PROJ * d) and w_out.shape == (1, 2 * d, d)
    tt = TILE
    win = _pack_rows(w_in[0])
    wout = _pack_rows(w_out[0])
    pw = _pack_rows(pool_w[0].reshape(N_GROUPS * POOL_GROUP, POOL_GROUP)).reshape(
        N_GROUPS, POOL_GROUP // ROWS_PER_WORD, POOL_GROUP)
    tri = jnp.asarray(np.tril(np.ones((CHUNK, CHUNK), np.float32)), dtype=BF16)
    lvl = jnp.asarray(_level_table(CHUNK))
    scratch_shapes = [
        pltpu.VMEM((HIST + tt, d), F32),
        pltpu.VMEM((N_HEADS, HEAD_DIM, HEAD_DIM), F32),
        pltpu.VMEM((tt, 2 * d), BF16),
        pltpu.VMEM((N_SECTIONS, tt, d), F32),
    ]

    return pl.pallas_call(
        _block_kernel,
        grid=(batch, seq // tt),
        in_specs=[
            pl.BlockSpec((1, tt, d), lambda b, j: (b, j, 0)),
            _const_spec((1, d)),
            _const_spec(win.shape),
            _const_spec(pw.shape),
            _const_spec((1, d)),
            _const_spec(lb_logits.shape),
            _const_spec((1, d)),
            _const_spec(wout.shape),
            _const_spec((1, d)),
            _const_spec(tri.shape),
            _const_spec(lvl.shape),
        ],
        out_specs=pl.BlockSpec((1, tt, d), lambda b, j: (b, j, 0)),
        out_shape=jax.ShapeDtypeStruct(x.shape, x.dtype),
        scratch_shapes=scratch_shapes,
        compiler_params=pltpu.CompilerParams(
            dimension_semantics=("arbitrary", "arbitrary"),
            vmem_limit_bytes=_vmem_limit_bytes((win, wout, pw, tri, lvl), scratch_shapes)),
        name="hybrid_pool_hgrn2_block",
    )(x, norm1_g, win, pw, pool_scale, lb_logits, rec_norm_g, wout, final_norm_g.reshape(1, d), tri, lvl)
```
